```python
import math
import jax, jax.numpy as jnp
from jax import lax
import numpy as np

D_MODEL = 1024
BATCH = 16
SEQ = 2048
DEPTH = 1
DEC_BATCH = 128
DEC_SEQ = 1
PAST_LEN = 16384
PAGE_SIZE = 128

A_HEADS = 8
A_KV_HEADS = 4
A_HEAD_DIM = 64
A_WIDTH = A_HEADS * A_HEAD_DIM
IDX_HEADS = 8
IDX_DIM = 64
IDX_TOPK = 256
B_HEADS = 8
B_NOPE = 64
B_ROPE = 32
B_QK = B_NOPE + B_ROPE
B_VDIM = 64
B_Q_LORA = 384
B_KV_LORA = 256
B_WIDTH = B_HEADS * B_VDIM

ROPE_THETA = 10000.0
NORM_EPS = 1e-6
Q_BLOCK = 128
A_SCALE = A_HEAD_DIM ** -0.5
B_SCALE = B_QK ** -0.5
IDX_SCALE = IDX_DIM ** -0.5
IDX_W_SCALE = IDX_HEADS ** -0.5

IN_SIZES = (
    A_HEADS * A_HEAD_DIM,
    A_KV_HEADS * A_HEAD_DIM,
    A_KV_HEADS * A_HEAD_DIM,
    IDX_HEADS * IDX_DIM,
    IDX_DIM,
    IDX_HEADS,
    A_WIDTH,
    B_Q_LORA,
    B_KV_LORA,
    B_ROPE,
    B_WIDTH,
    D_MODEL,
    D_MODEL,
)
IN_TOTAL = sum(IN_SIZES)

kernel_name = "dsa_mla_gated_parallel_decoder_step"


def _rms_norm(x, g):
    xf = x.astype(jnp.float32)
    y = xf * lax.rsqrt(jnp.mean(xf * xf, axis=-1, keepdims=True) + NORM_EPS)
    return (y * g.astype(jnp.float32)).astype(x.dtype)


def _rope(x, pos):
    half = x.shape[-1] // 2
    inv = jnp.power(ROPE_THETA, -jnp.arange(half, dtype=jnp.float32) / half)
    ang = pos.astype(jnp.float32)[:, None] * inv[None, :]
    cos = jnp.cos(ang)[None, :, None, :]
    sin = jnp.sin(ang)[None, :, None, :]
    xf = x.astype(jnp.float32)
    x1, x2 = xf[..., :half], xf[..., half:]
    return jnp.concatenate([x1 * cos - x2 * sin, x2 * cos + x1 * sin], axis=-1).astype(x.dtype)


def _split_cols(u):
    out, off = [], 0
    for n in IN_SIZES:
        out.append(u[..., off:off + n])
        off += n
    return out


def _project(x, pos, w):
    B, T, _ = x.shape
    h = _rms_norm(x, w["norm_w"])
    u = h @ w["w_in"]
    (q_a, k_a, v_a, iq, ik, iw, z_a, cq, ckv, kr, z_b, g_a, g_b) = _split_cols(u)
    q_a = _rope(_rms_norm(q_a.reshape(B, T, A_HEADS, A_HEAD_DIM), w["qn_a"]), pos)
    k_a = _rope(_rms_norm(k_a.reshape(B, T, A_KV_HEADS, A_HEAD_DIM), w["kn_a"]), pos)
    v_a = v_a.reshape(B, T, A_KV_HEADS, A_HEAD_DIM)
    iq = _rope(iq.reshape(B, T, IDX_HEADS, IDX_DIM), pos)
    ik = _rope(ik[:, :, None, :], pos)[:, :, 0]
    c_q = _rms_norm(cq, w["q_lora_norm"])
    q_b = _rms_norm((c_q @ w["w_uq"]).reshape(B, T, B_HEADS, B_QK), w["qn_b"])
    qn_b = q_b[..., :B_NOPE]
    qr_b = _rope(q_b[..., B_NOPE:], pos)
    c_kv = _rms_norm(ckv, w["kv_lora_norm"])
    k_r = _rope(_rms_norm(kr, w["kn_b"])[:, :, None, :], pos)[:, :, 0]
    return dict(q_a=q_a, k_a=k_a, v_a=v_a, iq=iq, ik=ik, iw=iw, z_a=z_a,
                qn_b=qn_b, qr_b=qr_b, c_kv=c_kv, k_r=k_r, z_b=z_b, g_a=g_a, g_b=g_b)


def _index_scores(iq, iw, ik):
    s = jnp.einsum('bthd,bsd->bths', iq, ik).astype(jnp.float32) * IDX_SCALE
    return jnp.einsum('bths,bth->bts', jax.nn.relu(s), iw.astype(jnp.float32) * IDX_W_SCALE)


def _take_rows(a, idx):
    return jax.vmap(lambda ab, ib: ab[ib])(a, idx)


def _sparse_attend(q, k_sel, v_sel, ok):
    B, T, H, Dh = q.shape
    G = H // A_KV_HEADS
    qg = q.reshape(B, T, A_KV_HEADS, G, Dh)
    s = jnp.einsum('btkgd,btnkd->btkgn', qg, k_sel).astype(jnp.float32) * A_SCALE
    s = jnp.where(ok[:, :, None, None, :], s, -jnp.inf)
    p = jax.nn.softmax(s, axis=-1).astype(v_sel.dtype)
    o = jnp.einsum('btkgn,btnkd->btkgd', p, v_sel)
    return o.reshape(B, T, H * Dh)


def _to_blocks(a):
    B, S = a.shape[:2]
    nb = S // Q_BLOCK
    return jnp.moveaxis(a.reshape(B, nb, Q_BLOCK, *a.shape[2:]), 1, 0)


def _from_blocks(o):
    nb, B, Qb = o.shape[:3]
    return jnp.moveaxis(o, 0, 1).reshape(B, nb * Qb, *o.shape[3:])


def _dsa_prompt(pr):
    k, v, ik = pr["k_a"], pr["v_a"], pr["ik"]
    S = k.shape[1]
    topk = min(IDX_TOPK, S // 4)
    kpos = jnp.arange(S, dtype=jnp.int32)

    def blk(args):
        qb, iqb, iwb, qpos = args
        sc = _index_scores(iqb, iwb, ik)
        sc = jnp.where(kpos[None, None, :] <= qpos[None, :, None], sc, -jnp.inf)
        _, sel = lax.top_k(sc, topk)
        ok = sel <= qpos[None, :, None]
        return _sparse_attend(qb, _take_rows(k, sel), _take_rows(v, sel), ok)

    qpos_b = kpos.reshape(S // Q_BLOCK, Q_BLOCK)
    o = lax.map(blk, (_to_blocks(pr["q_a"]), _to_blocks(pr["iq"]), _to_blocks(pr["iw"]), qpos_b))
    return _from_blocks(o)


def _mla_prompt(pr, w):
    c, kr = pr["c_kv"], pr["k_r"]
    S = c.shape[1]
    k_nope = jnp.einsum('bsc,chd->bshd', c, w["w_uk"])
    v = jnp.einsum('bsc,chd->bshd', c, w["w_uv"])
    kpos = jnp.arange(S, dtype=jnp.int32)

    def blk(args):
        qnb, qrb, qpos = args
        s = (jnp.einsum('bqhd,bshd->bhqs', qnb, k_nope)
             + jnp.einsum('bqhd,bsd->bhqs', qrb, kr)).astype(jnp.float32) * B_SCALE
        s = jnp.where(kpos[None, None, None, :] <= qpos[None, None, :, None], s, -jnp.inf)
        p = jax.nn.softmax(s, axis=-1).astype(v.dtype)
        o = jnp.einsum('bhqs,bshd->bqhd', p, v)
        return o.reshape(o.shape[0], o.shape[1], B_WIDTH)

    qpos_b = kpos.reshape(S // Q_BLOCK, Q_BLOCK)
    o = lax.map(blk, (_to_blocks(pr["qn_b"]), _to_blocks(pr["qr_b"]), qpos_b))
    return _from_blocks(o)


def _gather_pages(pool, page_table):
    g = pool[page_table]
    return g.reshape(page_table.shape[0], page_table.shape[1] * PAGE_SIZE, *pool.shape[2:])


def _dsa_sample(pr, cache_k, cache_v, cache_ik, page_table, pos):
    iq, iw, k_new, v_new, ik_new = pr["iq"], pr["iw"], pr["k_a"], pr["v_a"], pr["ik"]
    DB, T = iq.shape[:2]
    n_past = page_table.shape[1] * PAGE_SIZE
    L = n_past + T
    topk = min(IDX_TOPK, L // 4)
    ik_past = _gather_pages(cache_ik, page_table)
    sc = jnp.concatenate([_index_scores(iq, iw, ik_past), _index_scores(iq, iw, ik_new)], axis=-1)
    kpos = jnp.arange(L, dtype=jnp.int32)
    sc = jnp.where(kpos[None, None, :] <= pos[None, :, None], sc, -jnp.inf)
    _, sel = lax.top_k(sc, topk)
    ok = sel <= pos[None, :, None]
    from_past = (sel < n_past)[..., None, None]
    ip = jnp.minimum(sel, n_past - 1)
    page = jnp.take_along_axis(page_table, (ip // PAGE_SIZE).reshape(DB, -1), axis=1).reshape(ip.shape)
    phys = page * PAGE_SIZE + ip % PAGE_SIZE
    inew = jnp.clip(sel - n_past, 0, T - 1)
    k_flat = cache_k.reshape(-1, A_KV_HEADS, A_HEAD_DIM)
    v_flat = cache_v.reshape(-1, A_KV_HEADS, A_HEAD_DIM)
    k_sel = jnp.where(from_past, k_flat[phys].astype(k_new.dtype), _take_rows(k_new, inew))
    v_sel = jnp.where(from_past, v_flat[phys].astype(v_new.dtype), _take_rows(v_new, inew))
    return _sparse_attend(pr["q_a"], k_sel, v_sel, ok)


def _mla_sample(pr, cache_ckv, cache_kr, page_table, w):
    qn, qr, c_new, kr_new = pr["qn_b"], pr["qr_b"], pr["c_kv"], pr["k_r"]
    DB, T = qn.shape[:2]
    n_past = page_table.shape[1] * PAGE_SIZE
    c_past = _gather_pages(cache_ckv, page_table).astype(c_new.dtype)
    kr_past = _gather_pages(cache_kr, page_table).astype(kr_new.dtype)
    q_abs = jnp.einsum('bqhd,chd->bqhc', qn, w["w_uk"])
    s_past = (jnp.einsum('bqhc,bsc->bhqs', q_abs, c_past)
              + jnp.einsum('bqhd,bsd->bhqs', qr, kr_past)).astype(jnp.float32) * B_SCALE
    s_new = (jnp.einsum('bqhc,bsc->bhqs', q_abs, c_new)
             + jnp.einsum('bqhd,bsd->bhqs', qr, kr_new)).astype(jnp.float32) * B_SCALE
    tri = jnp.arange(T)[None, :] <= jnp.arange(T)[:, None]
    s_new = jnp.where(tri[None, None], s_new, -jnp.inf)
    p = jax.nn.softmax(jnp.concatenate([s_past, s_new], axis=-1), axis=-1).astype(c_new.dtype)
    o_lat = (jnp.einsum('bhqs,bsc->bqhc', p[..., :n_past], c_past)
             + jnp.einsum('bhqs,bsc->bqhc', p[..., n_past:], c_new))
    o = jnp.einsum('bqhc,chd->bqhd', o_lat, w["w_uv"])
    return o.reshape(DB, T, B_WIDTH)


def _merge(x, o_a, o_b, pr, w):
    a = (o_a * jax.nn.silu(pr["z_a"])) @ w["w_pa"]
    b = (o_b * jax.nn.silu(pr["z_b"])) @ w["w_pb"]
    m = jax.nn.sigmoid(pr["g_a"]) * a + jax.nn.sigmoid(pr["g_b"]) * b
    return x + m @ w["w_o"]


def setup_inputs(seed: int = 0) -> dict:
    key = jax.random.key(seed)
    ks = jax.random.split(key, 32)
    n_pages = PAST_LEN // PAGE_SIZE
    n_used = DEC_BATCH * n_pages
    n_pool = n_used + max(1, n_used // 4)

    def nrm(k, shape, scale=1.0):
        return jax.random.normal(k, shape, dtype=jnp.float32) * scale

    def gain(k, n):
        return 1.0 + 0.02 * jax.random.normal(k, (DEPTH, n), dtype=jnp.float32)

    page_table = jax.random.permutation(ks[7], n_pool)[:n_used].reshape(DEC_BATCH, n_pages).astype(jnp.int32)
    return {
        "x_prompt": nrm(ks[0], (BATCH, SEQ, D_MODEL)),
        "x_sample": nrm(ks[1], (DEC_BATCH, DEC_SEQ, D_MODEL)),
        "cache_k": nrm(ks[2], (DEPTH, n_pool, PAGE_SIZE, A_KV_HEADS, A_HEAD_DIM)),
        "cache_v": nrm(ks[3], (DEPTH, n_pool, PAGE_SIZE, A_KV_HEADS, A_HEAD_DIM)),
        "cache_idx_k": nrm(ks[4], (DEPTH, n_pool, PAGE_SIZE, IDX_DIM)),
        "cache_ckv": nrm(ks[5], (DEPTH, n_pool, PAGE_SIZE, B_KV_LORA)),
        "cache_kr": nrm(ks[6], (DEPTH, n_pool, PAGE_SIZE, B_ROPE)),
        "page_table": page_table,
        "norm_w": gain(ks[8], D_MODEL),
        "w_in": nrm(ks[9], (DEPTH, D_MODEL, IN_TOTAL), D_MODEL ** -0.5),
        "qn_a": gain(ks[10], A_HEAD_DIM),
        "kn_a": gain(ks[11], A_HEAD_DIM),
        "q_lora_norm": gain(ks[12], B_Q_LORA),
        "w_uq": nrm(ks[13], (DEPTH, B_Q_LORA, B_HEADS * B_QK), B_Q_LORA ** -0.5),
        "qn_b": gain(ks[14], B_QK),
        "kv_lora_norm": gain(ks[15], B_KV_LORA),
        "kn_b": gain(ks[16], B_ROPE),
        "w_uk": nrm(ks[17], (DEPTH, B_KV_LORA, B_HEADS, B_NOPE), B_KV_LORA ** -0.5),
        "w_uv": nrm(ks[18], (DEPTH, B_KV_LORA, B_HEADS, B_VDIM), B_KV_LORA ** -0.5),
        "w_pa": nrm(ks[19], (DEPTH, A_WIDTH, D_MODEL), A_WIDTH ** -0.5),
        "w_pb": nrm(ks[20], (DEPTH, B_WIDTH, D_MODEL), B_WIDTH ** -0.5),
        "w_o": nrm(ks[21], (DEPTH, D_MODEL, D_MODEL), D_MODEL ** -0.5),
    }


def reference(x_prompt, x_sample, cache_k, cache_v, cache_idx_k, cache_ckv, cache_kr, page_table,
              norm_w, w_in, qn_a, kn_a, q_lora_norm, w_uq, qn_b, kv_lora_norm, kn_b, w_uk, w_uv,
              w_pa, w_pb, w_o):
    x_p, x_s = x_prompt, x_sample
    n_past = page_table.shape[1] * PAGE_SIZE
    pos_p = jnp.arange(x_prompt.shape[1], dtype=jnp.int32)
    pos_s = n_past + jnp.arange(x_sample.shape[1], dtype=jnp.int32)
    kp, vp, ikp, cp, rp = [], [], [], [], []
    ksl, vsl, iks, cs, rs = [], [], [], [], []
    for l in range(DEPTH):
        w = dict(norm_w=norm_w[l], w_in=w_in[l], qn_a=qn_a[l], kn_a=kn_a[l],
                 q_lora_norm=q_lora_norm[l], w_uq=w_uq[l], qn_b=qn_b[l],
                 kv_lora_norm=kv_lora_norm[l], kn_b=kn_b[l], w_uk=w_uk[l], w_uv=w_uv[l],
                 w_pa=w_pa[l], w_pb=w_pb[l], w_o=w_o[l])
        pr = _project(x_p, pos_p, w)
        x_p = _merge(x_p, _dsa_prompt(pr), _mla_prompt(pr, w), pr, w)
        kp.append(pr["k_a"]); vp.append(pr["v_a"]); ikp.append(pr["ik"])
        cp.append(pr["c_kv"]); rp.append(pr["k_r"])
        ps = _project(x_s, pos_s, w)
        o_a = _dsa_sample(ps, cache_k[l], cache_v[l], cache_idx_k[l], page_table, pos_s)
        o_b = _mla_sample(ps, cache_ckv[l], cache_kr[l], page_table, w)
        x_s = _merge(x_s, o_a, o_b, ps, w)
        ksl.append(ps["k_a"]); vsl.append(ps["v_a"]); iks.append(ps["ik"])
        cs.append(ps["c_kv"]); rs.append(ps["k_r"])
    k_prompt = jnp.stack(kp); v_prompt = jnp.stack(vp); idx_k_prompt = jnp.stack(ikp)
    ckv_prompt = jnp.stack(cp); kr_prompt = jnp.stack(rp)
    k_sample = jnp.stack(ksl); v_sample = jnp.stack(vsl); idx_k_sample = jnp.stack(iks)
    ckv_sample = jnp.stack(cs); kr_sample = jnp.stack(rs)
    return (x_p, x_s, k_prompt, v_prompt, idx_k_prompt, ckv_prompt, kr_prompt,
            k_sample, v_sample, idx_k_sample, ckv_sample, kr_sample)
```

```python
import functools
import math

import jax
import jax.numpy as jnp
from jax import lax
from jax.experimental import pallas as pl
from jax.experimental.pallas import tpu as pltpu

F32 = jnp.float32
BF16 = jnp.bfloat16
I32 = jnp.int32

A_HEADS = 8
A_KV_HEADS = 4
A_HEAD_DIM = 64
IDX_HEADS = 8
IDX_DIM = 64
IDX_TOPK = 256
B_HEADS = 8
B_NOPE = 64
B_ROPE = 32
B_QK = B_NOPE + B_ROPE
B_VDIM = 64
B_Q_LORA = 384
B_KV_LORA = 256
PAGE_SIZE = 128
ROPE_THETA = 10000.0
NORM_EPS = 1e-6
A_SCALE = A_HEAD_DIM ** -0.5
B_SCALE = B_QK ** -0.5
IDX_SCALE = IDX_DIM ** -0.5
IDX_W_SCALE = IDX_HEADS ** -0.5

LANES = 128
VMEM_LIMIT = 56 * 1024 * 1024
NEG_INF = float("-inf")
INT_MIN = -2 ** 31
KEY_NEG_INF = -2 ** 31 + 0x7FFFFF
O_PERM = (0, 2, 1, 3, 4, 6, 5, 7)

_C_QA, _C_KA, _C_VA, _C_IQ, _C_IKD, _C_IW, _C_CQ, _C_CKV, _C_KR, _C_END = (
    0, 512, 768, 1024, 1536, 1664, 1792, 2176, 2432, 2560)


def _dot(a, b):
    return jnp.dot(a, b, preferred_element_type=F32)


def _dot_nt(a, b):
    return lax.dot_general(a, b, (((1,), (1,)), ((), ())), preferred_element_type=F32)


def _sortable_key(x):
    bits = lax.bitcast_convert_type(x, I32)
    key = jnp.where(bits < 0, bits ^ jnp.int32(0x7FFFFFFF), bits)
    return jnp.where(key == -1, 0, key)


def _proj_kernel(x_ref, c64_ref, s64_ref, cq_ref, sq_ref, nw_ref, w_ref, gq_ref, gk_ref,
                 gcq_ref, gckv_ref, gqb_ref, gkr_ref, wuq_ref, wuk_ref, wuv_ref, bd64_ref, ones_ref,
                 kf_ref, vf_ref, ikf_ref, ckvf_ref, krf_ref, iwf_ref,
                 qarr_ref, kbf_ref, vbf_ref, iq_ref, ikd_ref, qb_ref, kb_ref, vb_ref):
    tm = x_ref.shape[0]
    x = x_ref[...]
    ms = jnp.mean(x * x, axis=-1, keepdims=True)
    h = (x * lax.rsqrt(ms + NORM_EPS) * nw_ref[...]).astype(BF16)
    lane = lax.broadcasted_iota(I32, (tm, LANES), 1)
    lo_half = lane < 64
    c64, s64, cq, sq = c64_ref[...], s64_ref[...], cq_ref[...], sq_ref[...]
    bd64, ones = bd64_ref[...], ones_ref[...]

    def proj(a, b):
        return _dot(h, w_ref[:, a:b])

    def group_sum(s, m):
        hi = s.astype(BF16)
        lo = (s - hi.astype(F32)).astype(BF16)
        return _dot(hi, m) + _dot(lo, m)

    def rope(y, cos, sin, half):
        first = (lane % (2 * half)) < half
        sw = jnp.where(first, pltpu.roll(y, LANES - half, 1), pltpu.roll(y, half, 1))
        return y * cos + sw * sin

    def head_norm(uc, m, inv_n, g):
        ss = group_sum(uc * uc, m)
        return uc * lax.rsqrt(ss * inv_n + NORM_EPS) * g

    u = proj(_C_QA, _C_KA)
    for c in range(4):
        y = head_norm(u[:, c * LANES:(c + 1) * LANES], bd64, 1.0 / A_HEAD_DIM, gq_ref[...])
        y = rope(y, c64, s64, A_HEAD_DIM // 2) * A_SCALE
        r = pltpu.roll(y, 64, 1)
        if c % 2 == 0:
            he, ho = jnp.where(lo_half, y, 0.0), jnp.where(lo_half, r, 0.0)
        else:
            he, ho = jnp.where(lo_half, 0.0, r), jnp.where(lo_half, 0.0, y)
        qarr_ref[:, (2 * c) * LANES:(2 * c + 1) * LANES] = he.astype(BF16)
        qarr_ref[:, (2 * c + 1) * LANES:(2 * c + 2) * LANES] = ho.astype(BF16)

    u = proj(_C_KA, _C_VA)
    for c in range(2):
        y = head_norm(u[:, c * LANES:(c + 1) * LANES], bd64, 1.0 / A_HEAD_DIM, gk_ref[...])
        y = rope(y, c64, s64, A_HEAD_DIM // 2)
        kf_ref[:, c * LANES:(c + 1) * LANES] = y
        kbf_ref[:, c * LANES:(c + 1) * LANES] = y.astype(BF16)

    u = proj(_C_VA, _C_IQ)
    vf_ref[...] = u
    vbf_ref[...] = u.astype(BF16)

    u = proj(_C_IQ, _C_IKD)
    for c in range(4):
        y = rope(u[:, c * LANES:(c + 1) * LANES], c64, s64, IDX_DIM // 2) * IDX_SCALE
        iq_ref[:, c * LANES:(c + 1) * LANES] = y.astype(BF16)

    y = rope(proj(_C_IKD, _C_IW), c64, s64, IDX_DIM // 2)
    ikf_ref[...] = y[:, :IDX_DIM]
    ikd_ref[...] = y.astype(BF16)

    iwf_ref[...] = proj(_C_IW, _C_CQ) * IDX_W_SCALE

    u = proj(_C_CQ, _C_CKV)
    ms = jnp.mean(u * u, axis=-1, keepdims=True)
    cqn = (u * lax.rsqrt(ms + NORM_EPS) * gcq_ref[...]).astype(BF16)
    u = _dot(cqn, wuq_ref[...])
    for c in range(B_HEADS):
        y = head_norm(u[:, c * LANES:(c + 1) * LANES], ones, 1.0 / B_QK, gqb_ref[...])
        y = rope(y, cq, sq, B_ROPE // 2) * B_SCALE
        qb_ref[:, c * LANES:(c + 1) * LANES] = y.astype(BF16)

    u = proj(_C_CKV, _C_KR)
    ms = jnp.mean(u * u, axis=-1, keepdims=True)
    ckvn = u * lax.rsqrt(ms + NORM_EPS) * gckv_ref[...]
    ckvf_ref[...] = ckvn
    cb = ckvn.astype(BF16)
    vb_ref[...] = _dot(cb, wuv_ref[...]).astype(BF16)
    kn = _dot(cb, wuk_ref[...])
    u = proj(_C_KR, _C_END)
    y = head_norm(u, ones, 1.0 / B_ROPE, gkr_ref[...])
    y = rope(y, cq, sq, B_ROPE // 2)
    krf_ref[...] = y[:, B_NOPE:B_NOPE + B_ROPE]
    for c in range(B_HEADS):
        kb_ref[:, c * LANES:(c + 1) * LANES] = (kn[:, c * LANES:(c + 1) * LANES] + y).astype(BF16)


def _const_spec(shape):
    nd = len(shape)
    return pl.BlockSpec(shape, lambda *_: (0,) * nd, pipeline_mode=pl.Buffered(1))


def _project(x2d, tables, table_blocks, wts, tm):
    n, d = x2d.shape
    grid = (n // tm,)
    row = lambda w: pl.BlockSpec((tm, w), lambda i: (i, 0))
    tab = pl.BlockSpec((tm, LANES), lambda i: (i % table_blocks, 0))
    consts = [wts["norm_w"], wts["w_all"], wts["gq"], wts["gk"], wts["gcq"], wts["gckv"], wts["gqb"],
              wts["gkr"], wts["wuq"], wts["wuk"], wts["wuv"], wts["bd64"], wts["ones"]]
    out_widths = [(256, F32), (256, F32), (IDX_DIM, F32), (B_KV_LORA, F32), (B_ROPE, F32), (LANES, F32),
                  (1024, BF16), (256, BF16), (256, BF16), (512, BF16), (LANES, BF16),
                  (1024, BF16), (1024, BF16), (512, BF16)]
    outs = pl.pallas_call(
        _proj_kernel,
        grid=grid,
        in_specs=[row(d)] + [tab] * 4 + [_const_spec(c.shape) for c in consts],
        out_specs=[row(w) for w, _ in out_widths],
        out_shape=[jax.ShapeDtypeStruct((n, w), t) for w, t in out_widths],
        compiler_params=pltpu.CompilerParams(dimension_semantics=("arbitrary",),
                                             vmem_limit_bytes=VMEM_LIMIT),
        name="proj",
    )(x2d, *tables, *consts)
    names = ["k_f", "v_f", "ik_f", "ckv_f", "kr_f", "iw_f", "q_arr", "k_bf", "v_bf", "iq", "ikd",
             "q_b", "k_b", "v_b"]
    return dict(zip(names, outs))


def _sigmoid(z):
    return 1.0 / (1.0 + jnp.exp(-z))


def _merge_kernel(x_ref, oa_ref, ob_ref, nw_ref, wg_ref, wpa_ref, wpb_ref, wo_ref, y_ref):
    x = x_ref[...]
    ms = jnp.mean(x * x, axis=-1, keepdims=True)
    h = (x * lax.rsqrt(ms + NORM_EPS) * nw_ref[...]).astype(BF16)
    d = x.shape[1]
    za = _dot(h, wg_ref[:, 0:512])
    a = _dot((oa_ref[...] * (za * _sigmoid(za))).astype(BF16), wpa_ref[...])
    zb = _dot(h, wg_ref[:, 512:1024])
    b = _dot((ob_ref[...] * (zb * _sigmoid(zb))).astype(BF16), wpb_ref[...])
    m = _sigmoid(_dot(h, wg_ref[:, 1024:1024 + d])) * a
    m = m + _sigmoid(_dot(h, wg_ref[:, 1024 + d:1024 + 2 * d])) * b
    y_ref[...] = x + _dot(m.astype(BF16), wo_ref[...])


def _merge(x2d, o_a, o_b, wts, tm):
    n, d = x2d.shape
    row = lambda w: pl.BlockSpec((tm, w), lambda i: (i, 0))
    consts = [wts["norm_w"], wts["wg"], wts["wpa"], wts["wpb"], wts["wo"]]
    return pl.pallas_call(
        _merge_kernel,
        grid=(n // tm,),
        in_specs=[row(d), row(512), row(512)] + [_const_spec(c.shape) for c in consts],
        out_specs=row(d),
        out_shape=jax.ShapeDtypeStruct((n, d), F32),
        compiler_params=pltpu.CompilerParams(dimension_semantics=("arbitrary",),
                                             vmem_limit_bytes=VMEM_LIMIT),
        name="merge",
    )(x2d, o_a, o_b, *consts)


def _mla_prompt_kernel(qb_ref, kb_ref, vb_ref, o_ref, m_ref, l_ref, acc_ref):
    tq = qb_ref.shape[1]
    qi = pl.program_id(1)
    m_ref[...] = jnp.full(m_ref.shape, NEG_INF, F32)
    l_ref[...] = jnp.zeros(l_ref.shape, F32)
    acc_ref[...] = jnp.zeros(acc_ref.shape, F32)
    row = lax.broadcasted_iota(I32, (tq, tq), 0)
    col = lax.broadcasted_iota(I32, (tq, tq), 1)

    def step(kc, masked):
        rows = pl.ds(pl.multiple_of(kc * tq, tq), tq)
        for h in range(B_HEADS):
            q = qb_ref[0, :, h * LANES:(h + 1) * LANES]
            k = kb_ref[0, rows, h * LANES:(h + 1) * LANES]
            s = _dot_nt(q, k)
            if masked:
                s = jnp.where(col <= row, s, NEG_INF)
            m_old = m_ref[h]
            m_new = jnp.maximum(m_old, jnp.max(s, axis=-1, keepdims=True))
            p = jnp.exp(s - m_new)
            alpha = jnp.exp(m_old - m_new)
            l_ref[h] = alpha * l_ref[h] + jnp.sum(p, axis=-1, keepdims=True)
            v = vb_ref[0, rows, (h // 2) * LANES:(h // 2 + 1) * LANES]
            acc_ref[h] = alpha * acc_ref[h] + _dot(p.astype(BF16), v)
            m_ref[h] = m_new

    def body(kc, carry):
        step(kc, False)
        return carry

    lax.fori_loop(0, qi, body, 0)
    step(qi, True)
    lane = lax.broadcasted_iota(I32, (tq, LANES), 1)
    for c in range(B_HEADS // 2):
        r0 = acc_ref[2 * c] / l_ref[2 * c]
        r1 = acc_ref[2 * c + 1] / l_ref[2 * c + 1]
        o_ref[0, :, c * LANES:(c + 1) * LANES] = jnp.where(lane < 64, r0, r1)


def _mla_prompt(q_b, k_b, v_b, tq):
    bsz, s, _ = q_b.shape
    return pl.pallas_call(
        _mla_prompt_kernel,
        grid=(bsz, s // tq),
        in_specs=[pl.BlockSpec((1, tq, 1024), lambda b, i: (b, i, 0)),
                  pl.BlockSpec((1, s, 1024), lambda b, i: (b, 0, 0)),
                  pl.BlockSpec((1, s, 512), lambda b, i: (b, 0, 0))],
        out_specs=pl.BlockSpec((1, tq, 512), lambda b, i: (b, i, 0)),
        out_shape=jax.ShapeDtypeStruct((bsz, s, 512), F32),
        scratch_shapes=[pltpu.VMEM((B_HEADS, tq, 1), F32), pltpu.VMEM((B_HEADS, tq, 1), F32),
                        pltpu.VMEM((B_HEADS, tq, LANES), F32)],
        compiler_params=pltpu.CompilerParams(dimension_semantics=("arbitrary", "arbitrary"),
                                             vmem_limit_bytes=VMEM_LIMIT),
        name="mla_prompt",
    )(q_b, k_b, v_b)


def _dsa_prompt_kernel(qarr_ref, iq_ref, iw_ref, k_ref, v_ref, ikd_ref, o_ref,
                       key_ref, m_ref, l_ref, acc_ref, *, topk, idx_bits):
    tq = qarr_ref.shape[1]
    qi = pl.program_id(1)
    nk = qi + 1
    kf = float(topk)
    lane = lax.broadcasted_iota(I32, (tq, LANES), 1)
    row = lax.broadcasted_iota(I32, (tq, tq), 0)
    col = lax.broadcasted_iota(I32, (tq, tq), 1)
    iw = iw_ref[0]

    def idx_chunk(kc, masked):
        rows = pl.ds(pl.multiple_of(kc * tq, tq), tq)
        ik = ikd_ref[0, rows, :]
        tot = jnp.zeros((tq, tq), F32)
        for c in range(IDX_HEADS // 2):
            ch = iq_ref[0, :, c * LANES:(c + 1) * LANES]
            for e in range(2):
                lhs = jnp.where((lane < 64) if e == 0 else (lane >= 64), ch, jnp.zeros_like(ch))
                hh = 2 * c + e
                tot = tot + jnp.maximum(_dot_nt(lhs, ik), 0.0) * iw[:, hh:hh + 1]
        if masked:
            tot = jnp.where(col <= row, tot, NEG_INF)
        key_ref[kc] = _sortable_key(tot)

    def idx_body(kc, carry):
        idx_chunk(kc, False)
        return carry

    lax.fori_loop(0, qi, idx_body, 0)
    idx_chunk(qi, True)

    def count(pred_fn):
        def body(kc, acc):
            f = jnp.where(pred_fn(key_ref[kc], kc), 1.0, 0.0)
            part = f[:, 0:LANES]
            for j in range(1, tq // LANES):
                part = part + f[:, j * LANES:(j + 1) * LANES]
            return acc + part
        acc = lax.fori_loop(0, nk, body, jnp.zeros((tq, LANES), F32))
        return jnp.sum(acc, axis=1, keepdims=True)

    c0 = count(lambda blk, kc: blk >= 0)
    t = jnp.where(c0 >= kf, jnp.int32(0), jnp.int32(INT_MIN))

    def bit_body(i, t):
        cand = t | jnp.left_shift(jnp.int32(1), 30 - i)
        c = count(lambda blk, kc: blk >= cand)
        return jnp.where(c >= kf, cand, t)

    t = lax.fori_loop(0, 31, bit_body, t)

    cnt_gt = count(lambda blk, kc: blk > t)
    cnt_ge = count(lambda blk, kc: blk >= t)
    need = kf - cnt_gt
    excess = jnp.where((cnt_ge > kf) & (t > KEY_NEG_INF), 1.0, 0.0)
    any_excess = jnp.max(excess) > 0.0

    def tie_limit():
        def jbody(i, j):
            cand = j | jnp.left_shift(jnp.int32(1), idx_bits - 1 - i)
            c = count(lambda blk, kc: (blk == t) & ((kc * tq + col) < cand))
            return jnp.where(c <= need, cand, j)
        return lax.fori_loop(0, idx_bits, jbody, jnp.zeros((tq, 1), I32))

    jlim = lax.cond(any_excess, tie_limit, lambda: jnp.full((tq, 1), 2 ** idx_bits - 1, I32))

    m_ref[...] = jnp.full(m_ref.shape, NEG_INF, F32)
    l_ref[...] = jnp.zeros(l_ref.shape, F32)
    acc_ref[...] = jnp.zeros(acc_ref.shape, F32)

    def att_chunk(kc, masked):
        rows = pl.ds(pl.multiple_of(kc * tq, tq), tq)
        blk = key_ref[kc]
        sel = (blk > t) | ((blk == t) & ((kc * tq + col) < jlim))
        if masked:
            sel = sel & (col <= row)
        bias = jnp.where(sel, 0.0, NEG_INF)
        for h in range(A_HEADS):
            kv = slice((h // 4) * LANES, (h // 4 + 1) * LANES)
            s = _dot_nt(qarr_ref[0, :, h * LANES:(h + 1) * LANES], k_ref[0, rows, kv]) + bias
            m_old = m_ref[h]
            m_new = jnp.maximum(m_old, jnp.max(s, axis=-1, keepdims=True))
            m_safe = jnp.where(m_new == NEG_INF, 0.0, m_new)
            p = jnp.exp(s - m_safe)
            alpha = jnp.exp(m_old - m_safe)
            l_ref[h] = alpha * l_ref[h] + jnp.sum(p, axis=-1, keepdims=True)
            acc_ref[h] = alpha * acc_ref[h] + _dot(p.astype(BF16), v_ref[0, rows, kv])
            m_ref[h] = m_new

    def att_body(kc, carry):
        att_chunk(kc, False)
        return carry

    lax.fori_loop(0, qi, att_body, 0)
    att_chunk(qi, True)
    for i in range(A_HEADS // 2):
        h0, h1 = O_PERM[2 * i], O_PERM[2 * i + 1]
        r0 = acc_ref[h0] / l_ref[h0]
        r1 = acc_ref[h1] / l_ref[h1]
        o_ref[0, :, i * LANES:(i + 1) * LANES] = jnp.where(lane < 64, r0, r1)


def _dsa_prompt(q_arr, iq, iw, k_bf, v_bf, ikd, tq):
    bsz, s, _ = q_arr.shape
    topk = min(IDX_TOPK, s // 4)
    idx_bits = max(1, int(math.ceil(math.log2(s + 1))))
    qspec = lambda w: pl.BlockSpec((1, tq, w), lambda b, i: (b, i, 0))
    full = lambda w: pl.BlockSpec((1, s, w), lambda b, i: (b, 0, 0))
    return pl.pallas_call(
        functools.partial(_dsa_prompt_kernel, topk=topk, idx_bits=idx_bits),
        grid=(bsz, s // tq),
        in_specs=[qspec(1024), qspec(512), qspec(LANES), full(256), full(256), full(LANES)],
        out_specs=qspec(512),
        out_shape=jax.ShapeDtypeStruct((bsz, s, 512), F32),
        scratch_shapes=[pltpu.VMEM((s // tq, tq, tq), I32),
                        pltpu.VMEM((A_HEADS, tq, 1), F32), pltpu.VMEM((A_HEADS, tq, 1), F32),
                        pltpu.VMEM((A_HEADS, tq, LANES), F32)],
        compiler_params=pltpu.CompilerParams(dimension_semantics=("arbitrary", "arbitrary"),
                                             vmem_limit_bytes=VMEM_LIMIT),
        name="dsa_prompt",
    )(q_arr, iq, iw, k_bf, v_bf, ikd)


def _idx_sample_kernel(pt_ref, iws_ref, ikt_hbm, iqt_ref, iq_ref, iknew_ref, iwcol_ref,
                       sel_ref, selnew_ref, buf, qbc, sc_ref, sem, *, topk, idx_bits):
    b = pl.program_id(0)
    nb = pl.num_programs(0)
    n_pages = buf.shape[1]
    slot = b % 2
    kf = float(topk)

    def page_copy(bb, p, sl):
        return pltpu.make_async_copy(ikt_hbm.at[pt_ref[bb, p]], buf.at[sl, p], sem.at[sl])

    def fetch(bb, sl):
        def body(p, carry):
            page_copy(bb, p, sl).start()
            return carry
        lax.fori_loop(0, n_pages, body, 0)

    @pl.when(b == 0)
    def _():
        fetch(0, 0)

    @pl.when(b + 1 < nb)
    def _():
        fetch(b + 1, 1 - slot)

    def wait_body(p, carry):
        page_copy(b, p, slot).wait()
        return carry

    lax.fori_loop(0, n_pages, wait_body, 0)

    for h in range(IDX_HEADS):
        qbc[h] = jnp.broadcast_to(iqt_ref[0, h], (IDX_DIM, PAGE_SIZE))

    def page_body(p, carry):
        blk = buf[slot, p]
        tot = jnp.zeros((1, PAGE_SIZE), F32)
        for h in range(IDX_HEADS):
            s = jnp.sum(blk * qbc[h], axis=0, keepdims=True)
            tot = tot + jnp.maximum(s, 0.0) * iws_ref[b, h]
        sc_ref[pl.ds(p, 1), :] = tot
        return carry

    lax.fori_loop(0, n_pages, page_body, 0)

    s_new = jnp.sum(iq_ref[0] * iknew_ref[0], axis=1, keepdims=True)
    s_new = jnp.sum(jnp.maximum(s_new, 0.0) * iwcol_ref[0], axis=0, keepdims=True)

    keys = _sortable_key(sc_ref[...])
    knew = _sortable_key(s_new)
    pos = (lax.broadcasted_iota(I32, keys.shape, 0) * PAGE_SIZE
           + lax.broadcasted_iota(I32, keys.shape, 1))
    n_past = n_pages * PAGE_SIZE

    def total(f):
        return jnp.sum(jnp.sum(f, axis=0, keepdims=True), axis=1, keepdims=True)

    def count(pred_fn):
        return (total(jnp.where(pred_fn(keys, pos), 1.0, 0.0))
                + jnp.where(pred_fn(knew, n_past), 1.0, 0.0))

    c0 = count(lambda kk, ii: kk >= 0)
    t = jnp.where(c0 >= kf, jnp.int32(0), jnp.int32(INT_MIN))

    def bit_body(i, t):
        cand = t | jnp.left_shift(jnp.int32(1), 30 - i)
        return jnp.where(count(lambda kk, ii: kk >= cand) >= kf, cand, t)

    t = lax.fori_loop(0, 31, bit_body, t)
    need = kf - count(lambda kk, ii: kk > t)
    excess = count(lambda kk, ii: kk >= t) > kf

    def tie_limit():
        def jbody(i, j):
            cand = j | jnp.left_shift(jnp.int32(1), idx_bits - 1 - i)
            c = count(lambda kk, ii: (kk == t) & (ii < cand))
            return jnp.where(c <= need, cand, j)
        return lax.fori_loop(0, idx_bits, jbody, jnp.zeros((1, 1), I32))

    jlim = lax.cond(jnp.max(jnp.where(excess, 1.0, 0.0)) > 0.0, tie_limit,
                    lambda: jnp.full((1, 1), 2 ** idx_bits - 1, I32))
    sel_ref[0] = jnp.where((keys > t) | ((keys == t) & (pos < jlim)), 1.0, 0.0)
    sel_new = jnp.where((knew > t) | ((knew == t) & (n_past < jlim)), 1.0, 0.0)
    selnew_ref[0] = jnp.broadcast_to(sel_new, (1, LANES))


def _idx_sample(page_table, iw_s, ikt, iqt, iq3, ik_new, iw_col):
    db, n_pages = page_table.shape
    length = n_pages * PAGE_SIZE + 1
    topk = min(IDX_TOPK, length // 4)
    idx_bits = max(1, int(math.ceil(math.log2(length + 1))))
    grid_spec = pltpu.PrefetchScalarGridSpec(
        num_scalar_prefetch=2,
        grid=(db,),
        in_specs=[pl.BlockSpec(memory_space=pl.ANY),
                  pl.BlockSpec((1, IDX_HEADS, IDX_DIM, 1), lambda b, *_: (b, 0, 0, 0)),
                  pl.BlockSpec((1, IDX_HEADS, IDX_DIM), lambda b, *_: (b, 0, 0)),
                  pl.BlockSpec((1, 1, IDX_DIM), lambda b, *_: (b, 0, 0)),
                  pl.BlockSpec((1, IDX_HEADS, 1), lambda b, *_: (b, 0, 0))],
        out_specs=[pl.BlockSpec((1, n_pages, PAGE_SIZE), lambda b, *_: (b, 0, 0)),
                   pl.BlockSpec((1, 1, LANES), lambda b, *_: (b, 0, 0))],
        scratch_shapes=[pltpu.VMEM((2, n_pages, IDX_DIM, PAGE_SIZE), F32),
                        pltpu.VMEM((IDX_HEADS, IDX_DIM, PAGE_SIZE), F32),
                        pltpu.VMEM((n_pages, PAGE_SIZE), F32),
                        pltpu.SemaphoreType.DMA((2,))])
    return pl.pallas_call(
        functools.partial(_idx_sample_kernel, topk=topk, idx_bits=idx_bits),
        grid_spec=grid_spec,
        out_shape=[jax.ShapeDtypeStruct((db, n_pages, PAGE_SIZE), F32),
                   jax.ShapeDtypeStruct((db, 1, LANES), F32)],
        compiler_params=pltpu.CompilerParams(dimension_semantics=("arbitrary",),
                                             vmem_limit_bytes=VMEM_LIMIT),
        name="idx_sample",
    )(page_table, iw_s, ikt, iqt, iq3, ik_new, iw_col)


def _dsa_sample_kernel(pt_ref, kt_hbm, vt_hbm, qt_ref, q8_ref, sel_ref, selnew_ref, knew_ref, vnew_ref,
                       o_ref, kbuf, vbuf, qbc, s_ref, m_ref, l_ref, acc_ref, sem, *, ch):
    b = pl.program_id(0)
    c = pl.program_id(1)
    nb = pl.num_programs(0)
    nch = pl.num_programs(1)
    step = b * nch + c
    slot = step % 2

    def copies(bb, cc, sl, j):
        page = pt_ref[bb, cc * ch + j]
        return (pltpu.make_async_copy(kt_hbm.at[page], kbuf.at[sl, j], sem.at[0, sl]),
                pltpu.make_async_copy(vt_hbm.at[page], vbuf.at[sl, j], sem.at[1, sl]))

    def fetch(bb, cc, sl):
        for j in range(ch):
            ck, cv = copies(bb, cc, sl, j)
            ck.start()
            cv.start()

    @pl.when(step == 0)
    def _():
        fetch(0, 0, 0)

    @pl.when(step + 1 < nb * nch)
    def _():
        nxt = step + 1
        fetch(nxt // nch, nxt % nch, 1 - slot)

    @pl.when(c == 0)
    def _():
        m_ref[...] = jnp.full(m_ref.shape, NEG_INF, F32)
        l_ref[...] = jnp.zeros(l_ref.shape, F32)
        acc_ref[...] = jnp.zeros(acc_ref.shape, F32)
        for h in range(A_HEADS):
            qbc[h] = jnp.broadcast_to(qt_ref[0, h], (A_HEAD_DIM, PAGE_SIZE))

    for j in range(ch):
        ck, cv = copies(b, c, slot, j)
        ck.wait()
        cv.wait()

    for j in range(ch):
        bias = jnp.where(sel_ref[0, j:j + 1, :] > 0.5, 0.0, NEG_INF)
        for h in range(A_HEADS):
            s = jnp.sum(kbuf[slot, j, h // 2] * qbc[h], axis=0, keepdims=True)
            s_ref[j, h:h + 1, :] = s + bias
    s_all = s_ref[...]
    mc = jnp.max(jnp.max(s_all, axis=0), axis=1, keepdims=True)
    m_old = m_ref[...]
    m_new = jnp.maximum(m_old, mc)
    m_safe = jnp.where(m_new == NEG_INF, 0.0, m_new)
    alpha = jnp.exp(m_old - m_safe)
    m_ref[...] = m_new
    p_all = jnp.exp(s_all - m_safe[None])
    l_ref[...] = alpha * l_ref[...] + jnp.sum(jnp.sum(p_all, axis=0), axis=1, keepdims=True)
    s_ref[...] = p_all
    for h in range(A_HEADS):
        a = acc_ref[h] * alpha[h:h + 1, :]
        for j in range(ch):
            a = a + vbuf[slot, j, h // 2] * s_ref[j, h:h + 1, :]
        acc_ref[h] = a

    @pl.when(c == nch - 1)
    def _():
        m_old = m_ref[...]
        s_new = jnp.sum(q8_ref[0] * knew_ref[0], axis=1, keepdims=True)
        s_new = jnp.where(selnew_ref[0, :, 0:1] > 0.5, s_new, NEG_INF)
        m_new = jnp.maximum(m_old, s_new)
        alpha = jnp.exp(m_old - m_new)
        p_new = jnp.exp(s_new - m_new)
        denom = alpha * l_ref[...] + p_new
        for h in range(A_HEADS):
            num = (jnp.sum(acc_ref[h], axis=1, keepdims=True) * alpha[h:h + 1, :]
                   + vnew_ref[0, h] * p_new[h:h + 1, :])
            o_ref[0, h] = num / denom[h:h + 1, :]


def _dsa_sample(page_table, kt, vt, qt, q8, sel, sel_new, knew8, vnew8t, ch):
    db, n_pages = page_table.shape
    nch = n_pages // ch
    hd = A_HEAD_DIM
    grid_spec = pltpu.PrefetchScalarGridSpec(
        num_scalar_prefetch=1,
        grid=(db, nch),
        in_specs=[pl.BlockSpec(memory_space=pl.ANY), pl.BlockSpec(memory_space=pl.ANY),
                  pl.BlockSpec((1, A_HEADS, hd, 1), lambda b, c, *_: (b, 0, 0, 0)),
                  pl.BlockSpec((1, A_HEADS, hd), lambda b, c, *_: (b, 0, 0)),
                  pl.BlockSpec((1, ch, PAGE_SIZE), lambda b, c, *_: (b, c, 0)),
                  pl.BlockSpec((1, 1, LANES), lambda b, c, *_: (b, 0, 0)),
                  pl.BlockSpec((1, A_HEADS, hd), lambda b, c, *_: (b, 0, 0)),
                  pl.BlockSpec((1, A_HEADS, hd, 1), lambda b, c, *_: (b, 0, 0, 0))],
        out_specs=pl.BlockSpec((1, A_HEADS, hd, 1), lambda b, c, *_: (b, 0, 0, 0)),
        scratch_shapes=[pltpu.VMEM((2, ch, A_KV_HEADS, hd, PAGE_SIZE), F32),
                        pltpu.VMEM((2, ch, A_KV_HEADS, hd, PAGE_SIZE), F32),
                        pltpu.VMEM((A_HEADS, hd, PAGE_SIZE), F32),
                        pltpu.VMEM((ch, A_HEADS, PAGE_SIZE), F32),
                        pltpu.VMEM((A_HEADS, 1), F32), pltpu.VMEM((A_HEADS, 1), F32),
                        pltpu.VMEM((A_HEADS, hd, PAGE_SIZE), F32),
                        pltpu.SemaphoreType.DMA((2, 2))])
    return pl.pallas_call(
        functools.partial(_dsa_sample_kernel, ch=ch),
        grid_spec=grid_spec,
        out_shape=jax.ShapeDtypeStruct((db, A_HEADS, hd, 1), F32),
        compiler_params=pltpu.CompilerParams(dimension_semantics=("arbitrary", "arbitrary"),
                                             vmem_limit_bytes=VMEM_LIMIT),
        name="dsa_sample",
    )(page_table, kt, vt, qt, q8, sel, sel_new, knew8, vnew8t)


def _qabs_kernel(qb_ref, wukt_ref, o_ref):
    for h in range(B_HEADS):
        o_ref[h] = _dot(qb_ref[:, h * LANES:(h + 1) * LANES], wukt_ref[h]).astype(BF16)


def _qabs(q_b, wukt):
    db = q_b.shape[0]
    return pl.pallas_call(
        _qabs_kernel,
        out_shape=jax.ShapeDtypeStruct((B_HEADS, db, B_KV_LORA), BF16),
        name="mla_qabs",
    )(q_b, wukt)


def _ouv_kernel(olat_ref, wuv_ref, o_ref):
    db = olat_ref.shape[1]
    lane = lax.broadcasted_iota(I32, (db, LANES), 1)
    for c in range(B_HEADS // 2):
        w = wuv_ref[:, c * LANES:(c + 1) * LANES]
        r0 = _dot(olat_ref[2 * c].astype(BF16), w)
        r1 = _dot(olat_ref[2 * c + 1].astype(BF16), w)
        o_ref[:, c * LANES:(c + 1) * LANES] = jnp.where(lane < 64, r0, r1)


def _ouv(o_lat, wuv):
    db = o_lat.shape[1]
    return pl.pallas_call(
        _ouv_kernel,
        out_shape=jax.ShapeDtypeStruct((db, B_HEADS * B_VDIM), F32),
        name="mla_ouv",
    )(o_lat, wuv)


def _mla_sample_kernel(pt_ref, ckv_hbm, krt_hbm, qabs_ref, qrt_ref, qr_ref, cnew_ref, krnew_ref,
                       o_ref, cbuf, rbuf, qrbc, sr_ref, m_ref, l_ref, acc_ref, sem, *, ch):
    b = pl.program_id(0)
    c = pl.program_id(1)
    nb = pl.num_programs(0)
    nch = pl.num_programs(1)
    step = b * nch + c
    slot = step % 2

    def copies(bb, cc, sl, j):
        page = pt_ref[bb, cc * ch + j]
        return (pltpu.make_async_copy(ckv_hbm.at[page], cbuf.at[sl, pl.ds(j * PAGE_SIZE, PAGE_SIZE)],
                                      sem.at[0, sl]),
                pltpu.make_async_copy(krt_hbm.at[page], rbuf.at[sl, j], sem.at[1, sl]))

    def fetch(bb, cc, sl):
        for j in range(ch):
            cc_, cr_ = copies(bb, cc, sl, j)
            cc_.start()
            cr_.start()

    @pl.when(step == 0)
    def _():
        fetch(0, 0, 0)

    @pl.when(step + 1 < nb * nch)
    def _():
        nxt = step + 1
        fetch(nxt // nch, nxt % nch, 1 - slot)

    @pl.when(c == 0)
    def _():
        m_ref[...] = jnp.full(m_ref.shape, NEG_INF, F32)
        l_ref[...] = jnp.zeros(l_ref.shape, F32)
        acc_ref[...] = jnp.zeros(acc_ref.shape, F32)
        for h in range(B_HEADS):
            qrbc[h] = jnp.broadcast_to(qrt_ref[0, h], (B_ROPE, PAGE_SIZE))

    for j in range(ch):
        cc_, cr_ = copies(b, c, slot, j)
        cc_.wait()
        cr_.wait()

    for j in range(ch):
        blk = rbuf[slot, j]
        for h in range(B_HEADS):
            sr_ref[h:h + 1, j * PAGE_SIZE:(j + 1) * PAGE_SIZE] = jnp.sum(blk * qrbc[h], axis=0, keepdims=True)
    cb = cbuf[slot].astype(BF16)
    s = _dot_nt(qabs_ref[0], cb) + sr_ref[...]
    m_old = m_ref[...]
    m_new = jnp.maximum(m_old, jnp.max(s, axis=1, keepdims=True))
    alpha = jnp.exp(m_old - m_new)
    p = jnp.exp(s - m_new)
    l_ref[...] = alpha * l_ref[...] + jnp.sum(p, axis=1, keepdims=True)
    acc_ref[...] = alpha * acc_ref[...] + _dot(p.astype(BF16), cb)
    m_ref[...] = m_new

    @pl.when(c == nch - 1)
    def _():
        m_old = m_ref[...]
        s_new = (jnp.sum(qabs_ref[0].astype(F32) * cnew_ref[0], axis=1, keepdims=True)
                 + jnp.sum(qr_ref[0] * krnew_ref[0], axis=1, keepdims=True))
        m_new = jnp.maximum(m_old, s_new)
        alpha = jnp.exp(m_old - m_new)
        p_new = jnp.exp(s_new - m_new)
        denom = alpha * l_ref[...] + p_new
        o_ref[0] = (alpha * acc_ref[...] + p_new * cnew_ref[0]) / denom


def _mla_sample(page_table, ckv_pages, krt, qabs, qrt, qr, c_new, kr_new, ch):
    db, n_pages = page_table.shape
    nch = n_pages // ch
    grid_spec = pltpu.PrefetchScalarGridSpec(
        num_scalar_prefetch=1,
        grid=(db, nch),
        in_specs=[pl.BlockSpec(memory_space=pl.ANY), pl.BlockSpec(memory_space=pl.ANY),
                  pl.BlockSpec((1, B_HEADS, B_KV_LORA), lambda b, c, *_: (b, 0, 0)),
                  pl.BlockSpec((1, B_HEADS, B_ROPE, 1), lambda b, c, *_: (b, 0, 0, 0)),
                  pl.BlockSpec((1, B_HEADS, B_ROPE), lambda b, c, *_: (b, 0, 0)),
                  pl.BlockSpec((1, 1, B_KV_LORA), lambda b, c, *_: (b, 0, 0)),
                  pl.BlockSpec((1, 1, B_ROPE), lambda b, c, *_: (b, 0, 0))],
        out_specs=pl.BlockSpec((1, B_HEADS, B_KV_LORA), lambda b, c, *_: (b, 0, 0)),
        scratch_shapes=[pltpu.VMEM((2, ch * PAGE_SIZE, B_KV_LORA), F32),
                        pltpu.VMEM((2, ch, B_ROPE, PAGE_SIZE), F32),
                        pltpu.VMEM((B_HEADS, B_ROPE, PAGE_SIZE), F32),
                        pltpu.VMEM((B_HEADS, ch * PAGE_SIZE), F32),
                        pltpu.VMEM((B_HEADS, 1), F32), pltpu.VMEM((B_HEADS, 1), F32),
                        pltpu.VMEM((B_HEADS, B_KV_LORA), F32),
                        pltpu.SemaphoreType.DMA((2, 2))])
    return pl.pallas_call(
        functools.partial(_mla_sample_kernel, ch=ch),
        grid_spec=grid_spec,
        out_shape=jax.ShapeDtypeStruct((db, B_HEADS, B_KV_LORA), F32),
        compiler_params=pltpu.CompilerParams(dimension_semantics=("arbitrary", "arbitrary"),
                                             vmem_limit_bytes=VMEM_LIMIT),
        name="mla_sample",
    )(page_table, ckv_pages, krt, qabs, qrt, qr, c_new, kr_new)


def _rope_tables(pos):
    pos = pos.astype(F32)[:, None]
    n = pos.shape[0]

    def cs(half):
        inv = jnp.power(ROPE_THETA, -jnp.arange(half, dtype=F32) / half)
        ang = pos * inv[None, :]
        return jnp.cos(ang), jnp.sin(ang)

    c, s = cs(A_HEAD_DIM // 2)
    c64 = jnp.tile(jnp.concatenate([c, c], axis=1), (1, 2))
    s64 = jnp.tile(jnp.concatenate([-s, s], axis=1), (1, 2))
    c, s = cs(B_ROPE // 2)
    one, zero = jnp.ones((n, B_NOPE), F32), jnp.zeros((n, B_NOPE), F32)
    cq = jnp.concatenate([one, c, c, one[:, :32]], axis=1)
    sq = jnp.concatenate([zero, -s, s, zero[:, :32]], axis=1)
    return c64, s64, cq, sq


def _pack_weights(norm_w, w_in, qn_a, kn_a, q_lora_norm, w_uq, qn_b, kv_lora_norm, kn_b, w_uk, w_uv,
                  w_pa, w_pb, w_o):
    d = w_in.shape[0]
    sizes = (512, 256, 256, 512, IDX_DIM, IDX_HEADS, 512, B_Q_LORA, B_KV_LORA, B_ROPE, 512, d, d)
    offs = [0]
    for n in sizes:
        offs.append(offs[-1] + n)
    (w_qa, w_ka, w_va, w_iq, w_ik, w_iw, w_za, w_cq, w_ckv, w_kr, w_zb, w_ga, w_gb) = [
        w_in[:, offs[i]:offs[i + 1]] for i in range(len(sizes))]
    zc = lambda n: jnp.zeros((d, n), F32)
    w_all = jnp.concatenate(
        [w_qa, w_ka, w_va, w_iq, w_ik, w_ik, w_iw, zc(LANES - IDX_HEADS), w_cq, w_ckv,
         zc(B_NOPE), w_kr, zc(LANES - B_NOPE - B_ROPE)], axis=1).astype(BF16)
    perm = jnp.array(O_PERM)
    w_za_p = w_za.reshape(d, A_HEADS, A_HEAD_DIM)[:, perm].reshape(d, 512)
    wg = jnp.concatenate([w_za_p, w_zb, w_ga, w_gb], axis=1).astype(BF16)
    wpa = w_pa.reshape(A_HEADS, A_HEAD_DIM, d)[perm].reshape(512, d).astype(BF16)
    wuq3 = w_uq.reshape(B_Q_LORA, B_HEADS, B_QK)
    wuq = jnp.pad(wuq3, ((0, 0), (0, 0), (0, LANES - B_QK))).reshape(B_Q_LORA, B_HEADS * LANES).astype(BF16)
    wuk = jnp.pad(w_uk, ((0, 0), (0, 0), (0, LANES - B_NOPE))).reshape(B_KV_LORA, B_HEADS * LANES).astype(BF16)
    wukt = jnp.pad(jnp.transpose(w_uk, (1, 2, 0)), ((0, 0), (0, LANES - B_NOPE), (0, 0))).astype(BF16)
    wuv = w_uv.reshape(B_KV_LORA, B_HEADS * B_VDIM).astype(BF16)
    z32 = jnp.zeros((32,), F32)
    r = jnp.arange(LANES)
    return dict(
        norm_w=norm_w.reshape(1, d), w_all=w_all, wg=wg, wpa=wpa, wpb=w_pb.astype(BF16), wo=w_o.astype(BF16),
        gq=jnp.tile(qn_a, 2).reshape(1, LANES), gk=jnp.tile(kn_a, 2).reshape(1, LANES),
        gcq=q_lora_norm.reshape(1, B_Q_LORA), gckv=kv_lora_norm.reshape(1, B_KV_LORA),
        gqb=jnp.concatenate([qn_b, z32]).reshape(1, LANES),
        gkr=jnp.concatenate([jnp.zeros((B_NOPE,), F32), kn_b, z32]).reshape(1, LANES),
        wuq=wuq, wuk=wuk, wukt=wukt, wuv=wuv,
        bd64=(r[:, None] // 64 == r[None, :] // 64).astype(BF16),
        ones=jnp.ones((LANES, LANES), BF16))


def _layer(x_p, x_s, cache_k, cache_v, cache_ik, cache_ckv, cache_kr, page_table, wts):
    bsz, seq, d = x_p.shape
    db, dec_seq, _ = x_s.shape
    assert dec_seq == 1, "the sample group decodes one token per sequence"
    n_pages = page_table.shape[1]
    n_past = n_pages * PAGE_SIZE
    tq = min(256, seq)

    tabs_p = _rope_tables(jnp.arange(seq, dtype=I32))
    pr = _project(x_p.reshape(bsz * seq, d), tabs_p, seq // tq, wts, tq)
    r3 = lambda a: a.reshape(bsz, seq, a.shape[-1])
    o_a = _dsa_prompt(r3(pr["q_arr"]), r3(pr["iq"]), r3(pr["iw_f"]), r3(pr["k_bf"]), r3(pr["v_bf"]),
                      r3(pr["ikd"]), tq)
    o_b = _mla_prompt(r3(pr["q_b"]), r3(pr["k_b"]), r3(pr["v_b"]), tq)
    y_p = _merge(x_p.reshape(bsz * seq, d), o_a.reshape(bsz * seq, 512), o_b.reshape(bsz * seq, 512),
                 wts, tq).reshape(bsz, seq, d)

    tabs_s = _rope_tables(jnp.full((db,), n_past, dtype=I32))
    ps = _project(x_s.reshape(db, d), tabs_s, 1, wts, db)
    ikt = jnp.swapaxes(cache_ik, 1, 2)
    krt = jnp.swapaxes(cache_kr, 1, 2)
    kt = jnp.transpose(cache_k, (0, 2, 3, 1))
    vt = jnp.transpose(cache_v, (0, 2, 3, 1))
    iq3 = ps["iq"].astype(F32).reshape(db, IDX_HEADS, IDX_DIM)
    iw_s = ps["iw_f"][:, :IDX_HEADS]
    sel, sel_new = _idx_sample(page_table, iw_s, ikt, iq3[..., None], iq3,
                               ps["ik_f"].reshape(db, 1, IDX_DIM), iw_s[..., None])
    q4 = ps["q_arr"].astype(F32).reshape(db, A_HEADS, 2, 64)
    q8 = q4[:, :, 0] + q4[:, :, 1]
    knew8 = jnp.repeat(ps["k_f"].reshape(db, A_KV_HEADS, A_HEAD_DIM), A_HEADS // A_KV_HEADS, axis=1)
    vnew8 = jnp.repeat(ps["v_f"].reshape(db, A_KV_HEADS, A_HEAD_DIM), A_HEADS // A_KV_HEADS, axis=1)
    o_as = _dsa_sample(page_table, kt, vt, q8[..., None], q8, sel, sel_new, knew8, vnew8[..., None],
                       ch=min(8, n_pages))
    o_as = o_as.reshape(db, A_HEADS, A_HEAD_DIM)[:, jnp.array(O_PERM)].reshape(db, 512)

    qabs = jnp.swapaxes(_qabs(ps["q_b"], wts["wukt"]), 0, 1)
    qr = ps["q_b"].astype(F32).reshape(db, B_HEADS, LANES)[:, :, B_NOPE:B_NOPE + B_ROPE]
    o_lat = _mla_sample(page_table, cache_ckv, krt, qabs, qr[..., None], qr,
                        ps["ckv_f"].reshape(db, 1, B_KV_LORA), ps["kr_f"].reshape(db, 1, B_ROPE),
                        ch=min(16, n_pages))
    o_bs = _ouv(jnp.swapaxes(o_lat, 0, 1), wts["wuv"])
    y_s = _merge(x_s.reshape(db, d), o_as, o_bs, wts, db).reshape(db, 1, d)

    new_p = (pr["k_f"].reshape(bsz, seq, A_KV_HEADS, A_HEAD_DIM), pr["v_f"].reshape(bsz, seq, A_KV_HEADS, A_HEAD_DIM),
             pr["ik_f"].reshape(bsz, seq, IDX_DIM), pr["ckv_f"].reshape(bsz, seq, B_KV_LORA),
             pr["kr_f"].reshape(bsz, seq, B_ROPE))
    new_s = (ps["k_f"].reshape(db, 1, A_KV_HEADS, A_HEAD_DIM), ps["v_f"].reshape(db, 1, A_KV_HEADS, A_HEAD_DIM),
             ps["ik_f"].reshape(db, 1, IDX_DIM), ps["ckv_f"].reshape(db, 1, B_KV_LORA),
             ps["kr_f"].reshape(db, 1, B_ROPE))
    return y_p, y_s, new_p, new_s


def kernel(x_prompt, x_sample, cache_k, cache_v, cache_idx_k, cache_ckv, cache_kr, page_table, norm_w, w_in, qn_a, kn_a, q_lora_norm, w_uq, qn_b, kv_lora_norm, kn_b, w_uk, w_uv, w_pa, w_pb, w_o):
    depth = w_in.shape[0]
    x_p, x_s = x_prompt, x_sample
    outs_p, outs_s = [], []
    for l in range(depth):
        wts = _pack_weights(norm_w[l], w_in[l], qn_a[l], kn_a[l], q_lora_norm[l], w_uq[l], qn_b[l],
                            kv_lora_norm[l], kn_b[l], w_uk[l], w_uv[l], w_pa[l], w_pb[l], w_o[l])
        x_p, x_s, new_p, new_s = _layer(x_p, x_s, cache_k[l], cache_v[l], cache_idx_k[l], cache_ckv[l],
                                        cache_kr[l], page_table, wts)
        outs_p.append(new_p)
        outs_s.append(new_s)
    stack = lambda outs, i: jnp.stack([o[i] for o in outs])
    return (x_p, x_s) + tuple(stack(outs_p, i) for i in range(5)) + tuple(stack(outs_s, i) for i in range(5))
```

```python
import functools
import math

import jax
import jax.numpy as jnp
from jax import lax
from jax.experimental import pallas as pl
from jax.experimental.pallas import tpu as pltpu

F32 = jnp.float32
BF16 = jnp.bfloat16
I32 = jnp.int32

A_HEADS = 8
A_KV_HEADS = 4
A_HEAD_DIM = 64
IDX_HEADS = 8
IDX_DIM = 64
IDX_TOPK = 256
B_HEADS = 8
B_NOPE = 64
B_ROPE = 32
B_QK = B_NOPE + B_ROPE
B_VDIM = 64
B_Q_LORA = 384
B_KV_LORA = 256
PAGE_SIZE = 128
ROPE_THETA = 10000.0
NORM_EPS = 1e-6
A_SCALE = A_HEAD_DIM ** -0.5
B_SCALE = B_QK ** -0.5
IDX_SCALE = IDX_DIM ** -0.5
IDX_W_SCALE = IDX_HEADS ** -0.5

LANES = 128
VMEM_LIMIT = 56 * 1024 * 1024
NEG_INF = float("-inf")
INT_MIN = -2 ** 31
KEY_NEG_INF = -2 ** 31 + 0x7FFFFF
O_PERM = (0, 2, 1, 3, 4, 6, 5, 7)

_C_QA, _C_KA, _C_VA, _C_IQ, _C_IKD, _C_IW, _C_CQ, _C_CKV, _C_KR, _C_END = (
    0, 512, 768, 1024, 1536, 1664, 1792, 2176, 2432, 2560)


def _dot(a, b):
    return jnp.dot(a, b, preferred_element_type=F32)


def _dot_nt(a, b):
    return lax.dot_general(a, b, (((1,), (1,)), ((), ())), preferred_element_type=F32)


def _sortable_key(x):
    bits = lax.bitcast_convert_type(x, I32)
    key = jnp.where(bits < 0, bits ^ jnp.int32(0x7FFFFFFF), bits)
    return jnp.where(key == -1, 0, key)


def _proj_kernel(x_ref, c64_ref, s64_ref, cq_ref, sq_ref, nw_ref, w_ref, gq_ref, gk_ref,
                 gcq_ref, gckv_ref, gqb_ref, gkr_ref, wuq_ref, wuk_ref, wuv_ref, bd64_ref, ones_ref,
                 kf_ref, vf_ref, ikf_ref, ckvf_ref, krf_ref, iwf_ref,
                 qarr_ref, kbf_ref, vbf_ref, iq_ref, ikd_ref, qb_ref, kb_ref, vb_ref):
    tm = x_ref.shape[0]
    x = x_ref[...]
    ms = jnp.mean(x * x, axis=-1, keepdims=True)
    h = (x * lax.rsqrt(ms + NORM_EPS) * nw_ref[...]).astype(BF16)
    lane = lax.broadcasted_iota(I32, (tm, LANES), 1)
    lo_half = lane < 64
    c64, s64, cq, sq = c64_ref[...], s64_ref[...], cq_ref[...], sq_ref[...]
    bd64, ones = bd64_ref[...], ones_ref[...]

    def proj(a, b):
        return _dot(h, w_ref[:, a:b])

    def group_sum(s, m):
        hi = s.astype(BF16)
        lo = (s - hi.astype(F32)).astype(BF16)
        return _dot(hi, m) + _dot(lo, m)

    def rope(y, cos, sin, half):
        first = (lane % (2 * half)) < half
        sw = jnp.where(first, pltpu.roll(y, LANES - half, 1), pltpu.roll(y, half, 1))
        return y * cos + sw * sin

    def head_norm(uc, m, inv_n, g):
        ss = group_sum(uc * uc, m)
        return uc * lax.rsqrt(ss * inv_n + NORM_EPS) * g

    u = proj(_C_QA, _C_KA)
    for c in range(4):
        y = head_norm(u[:, c * LANES:(c + 1) * LANES], bd64, 1.0 / A_HEAD_DIM, gq_ref[...])
        y = rope(y, c64, s64, A_HEAD_DIM // 2) * A_SCALE
        r = pltpu.roll(y, 64, 1)
        if c % 2 == 0:
            he, ho = jnp.where(lo_half, y, 0.0), jnp.where(lo_half, r, 0.0)
        else:
            he, ho = jnp.where(lo_half, 0.0, r), jnp.where(lo_half, 0.0, y)
        qarr_ref[:, (2 * c) * LANES:(2 * c + 1) * LANES] = he.astype(BF16)
        qarr_ref[:, (2 * c + 1) * LANES:(2 * c + 2) * LANES] = ho.astype(BF16)

    u = proj(_C_KA, _C_VA)
    for c in range(2):
        y = head_norm(u[:, c * LANES:(c + 1) * LANES], bd64, 1.0 / A_HEAD_DIM, gk_ref[...])
        y = rope(y, c64, s64, A_HEAD_DIM // 2)
        kf_ref[:, c * LANES:(c + 1) * LANES] = y
        kbf_ref[:, c * LANES:(c + 1) * LANES] = y.astype(BF16)

    u = proj(_C_VA, _C_IQ)
    vf_ref[...] = u
    vbf_ref[...] = u.astype(BF16)

    u = proj(_C_IQ, _C_IKD)
    for c in range(4):
        y = rope(u[:, c * LANES:(c + 1) * LANES], c64, s64, IDX_DIM // 2) * IDX_SCALE
        iq_ref[:, c * LANES:(c + 1) * LANES] = y.astype(BF16)

    y = rope(proj(_C_IKD, _C_IW), c64, s64, IDX_DIM // 2)
    ikf_ref[...] = y[:, :IDX_DIM]
    ikd_ref[...] = y.astype(BF16)

    iwf_ref[...] = proj(_C_IW, _C_CQ) * IDX_W_SCALE

    u = proj(_C_CQ, _C_CKV)
    ms = jnp.mean(u * u, axis=-1, keepdims=True)
    cqn = (u * lax.rsqrt(ms + NORM_EPS) * gcq_ref[...]).astype(BF16)
    u = _dot(cqn, wuq_ref[...])
    for c in range(B_HEADS):
        y = head_norm(u[:, c * LANES:(c + 1) * LANES], ones, 1.0 / B_QK, gqb_ref[...])
        y = rope(y, cq, sq, B_ROPE // 2) * B_SCALE
        qb_ref[:, c * LANES:(c + 1) * LANES] = y.astype(BF16)

    u = proj(_C_CKV, _C_KR)
    ms = jnp.mean(u * u, axis=-1, keepdims=True)
    ckvn = u * lax.rsqrt(ms + NORM_EPS) * gckv_ref[...]
    ckvf_ref[...] = ckvn
    cb = ckvn.astype(BF16)
    vb_ref[...] = _dot(cb, wuv_ref[...]).astype(BF16)
    kn = _dot(cb, wuk_ref[...])
    u = proj(_C_KR, _C_END)
    y = head_norm(u, ones, 1.0 / B_ROPE, gkr_ref[...])
    y = rope(y, cq, sq, B_ROPE // 2)
    krf_ref[...] = y[:, B_NOPE:B_NOPE + B_ROPE]
    for c in range(B_HEADS):
        kb_ref[:, c * LANES:(c + 1) * LANES] = (kn[:, c * LANES:(c + 1) * LANES] + y).astype(BF16)


def _const_spec(shape):
    nd = len(shape)
    return pl.BlockSpec(shape, lambda *_: (0,) * nd, pipeline_mode=pl.Buffered(1))


def _project(x2d, tables, table_blocks, wts, tm):
    n, d = x2d.shape
    grid = (n // tm,)
    row = lambda w: pl.BlockSpec((tm, w), lambda i: (i, 0))
    tab = pl.BlockSpec((tm, LANES), lambda i: (i % table_blocks, 0))
    consts = [wts["norm_w"], wts["w_all"], wts["gq"], wts["gk"], wts["gcq"], wts["gckv"], wts["gqb"],
              wts["gkr"], wts["wuq"], wts["wuk"], wts["wuv"], wts["bd64"], wts["ones"]]
    out_widths = [(256, F32), (256, F32), (IDX_DIM, F32), (B_KV_LORA, F32), (B_ROPE, F32), (LANES, F32),
                  (1024, BF16), (256, BF16), (256, BF16), (512, BF16), (LANES, BF16),
                  (1024, BF16), (1024, BF16), (512, BF16)]
    outs = pl.pallas_call(
        _proj_kernel,
        grid=grid,
        in_specs=[row(d)] + [tab] * 4 + [_const_spec(c.shape) for c in consts],
        out_specs=[row(w) for w, _ in out_widths],
        out_shape=[jax.ShapeDtypeStruct((n, w), t) for w, t in out_widths],
        compiler_params=pltpu.CompilerParams(dimension_semantics=("arbitrary",),
                                             vmem_limit_bytes=VMEM_LIMIT),
        name="proj",
    )(x2d, *tables, *consts)
    names = ["k_f", "v_f", "ik_f", "ckv_f", "kr_f", "iw_f", "q_arr", "k_bf", "v_bf", "iq", "ikd",
             "q_b", "k_b", "v_b"]
    return dict(zip(names, outs))


def _sigmoid(z):
    return 1.0 / (1.0 + jnp.exp(-z))


def _merge_kernel(x_ref, oa_ref, ob_ref, nw_ref, wg_ref, wpa_ref, wpb_ref, wo_ref, y_ref):
    x = x_ref[...]
    ms = jnp.mean(x * x, axis=-1, keepdims=True)
    h = (x * lax.rsqrt(ms + NORM_EPS) * nw_ref[...]).astype(BF16)
    d = x.shape[1]
    za = _dot(h, wg_ref[:, 0:512])
    a = _dot((oa_ref[...] * (za * _sigmoid(za))).astype(BF16), wpa_ref[...])
    zb = _dot(h, wg_ref[:, 512:1024])
    b = _dot((ob_ref[...] * (zb * _sigmoid(zb))).astype(BF16), wpb_ref[...])
    m = _sigmoid(_dot(h, wg_ref[:, 1024:1024 + d])) * a
    m = m + _sigmoid(_dot(h, wg_ref[:, 1024 + d:1024 + 2 * d])) * b
    y_ref[...] = x + _dot(m.astype(BF16), wo_ref[...])


def _merge(x2d, o_a, o_b, wts, tm):
    n, d = x2d.shape
    row = lambda w: pl.BlockSpec((tm, w), lambda i: (i, 0))
    consts = [wts["norm_w"], wts["wg"], wts["wpa"], wts["wpb"], wts["wo"]]
    return pl.pallas_call(
        _merge_kernel,
        grid=(n // tm,),
        in_specs=[row(d), row(512), row(512)] + [_const_spec(c.shape) for c in consts],
        out_specs=row(d),
        out_shape=jax.ShapeDtypeStruct((n, d), F32),
        compiler_params=pltpu.CompilerParams(dimension_semantics=("arbitrary",),
                                             vmem_limit_bytes=VMEM_LIMIT),
        name="merge",
    )(x2d, o_a, o_b, *consts)


V_ONES_ROWS = 16
VT_ROWS = LANES + V_ONES_ROWS


def _online_softmax_step(s_ref, vt, h, m_ref, l_ref, acc_ref, guard):
    s = s_ref[h]
    m_old = m_ref[h]
    m_new = jnp.maximum(m_old, jnp.max(s, axis=0, keepdims=True))
    m_sub = jnp.where(m_new == NEG_INF, 0.0, m_new) if guard else m_new
    p = jnp.exp(s - m_sub)
    alpha = jnp.exp(m_old - m_sub)
    pv = _dot(vt, p.astype(BF16))
    acc_ref[h] = alpha * acc_ref[h] + pv[:LANES]
    l_ref[h] = alpha * l_ref[h] + pv[LANES:LANES + 1]
    m_ref[h] = m_new


def _finish_heads(o_ref, l_ref, acc_ref, pairs):
    tq = acc_ref.shape[2]
    rowi = lax.broadcasted_iota(I32, (LANES, tq), 0)
    for i, (h0, h1) in enumerate(pairs):
        r0 = acc_ref[h0] * (1.0 / l_ref[h0])
        r1 = acc_ref[h1] * (1.0 / l_ref[h1])
        o_ref[0, :, i * LANES:(i + 1) * LANES] = jnp.where(rowi < 64, r0, r1).T


def _mla_prompt_kernel(qb_ref, kb_ref, vbt_ref, o_ref, s_ref, m_ref, l_ref, acc_ref):
    tq = qb_ref.shape[1]
    qi = pl.program_id(1)
    m_ref[...] = jnp.full(m_ref.shape, NEG_INF, F32)
    l_ref[...] = jnp.zeros(l_ref.shape, F32)
    acc_ref[...] = jnp.zeros(acc_ref.shape, F32)
    krow = lax.broadcasted_iota(I32, (tq, tq), 0)
    qcol = lax.broadcasted_iota(I32, (tq, tq), 1)

    def step(kc, masked):
        rows = pl.ds(pl.multiple_of(kc * tq, tq), tq)
        for h in range(B_HEADS):
            s = _dot_nt(kb_ref[0, rows, h * LANES:(h + 1) * LANES], qb_ref[0, :, h * LANES:(h + 1) * LANES])
            if masked:
                s = jnp.where(krow <= qcol, s, NEG_INF)
            s_ref[h] = s
        for h in range(B_HEADS):
            _online_softmax_step(s_ref, vbt_ref[0, kc, h // 2], h, m_ref, l_ref, acc_ref, guard=False)

    def body(kc, carry):
        step(kc, False)
        return carry

    lax.fori_loop(0, qi, body, 0)
    step(qi, True)
    _finish_heads(o_ref, l_ref, acc_ref, [(2 * c, 2 * c + 1) for c in range(B_HEADS // 2)])


def _mla_prompt(q_b, k_b, vbt, tq):
    bsz, s, _ = q_b.shape
    nk = s // tq
    return pl.pallas_call(
        _mla_prompt_kernel,
        grid=(bsz, nk),
        in_specs=[pl.BlockSpec((1, tq, 1024), lambda b, i: (b, i, 0)),
                  pl.BlockSpec((1, s, 1024), lambda b, i: (b, 0, 0)),
                  pl.BlockSpec((1, nk, 4, VT_ROWS, tq), lambda b, i: (b, 0, 0, 0, 0))],
        out_specs=pl.BlockSpec((1, tq, 512), lambda b, i: (b, i, 0)),
        out_shape=jax.ShapeDtypeStruct((bsz, s, 512), F32),
        scratch_shapes=[pltpu.VMEM((B_HEADS, tq, tq), F32),
                        pltpu.VMEM((B_HEADS, 1, tq), F32), pltpu.VMEM((B_HEADS, 1, tq), F32),
                        pltpu.VMEM((B_HEADS, LANES, tq), F32)],
        compiler_params=pltpu.CompilerParams(dimension_semantics=("arbitrary", "arbitrary"),
                                             vmem_limit_bytes=VMEM_LIMIT),
        name="mla_prompt",
    )(q_b, k_b, vbt)


def _dsa_prompt_kernel(qarr_ref, iq_ref, iwt_ref, k_ref, vt_ref, ikd_ref, o_ref,
                       key_ref, iqm_ref, s_ref, m_ref, l_ref, acc_ref, *, topk, idx_bits):
    tq = qarr_ref.shape[1]
    qi = pl.program_id(1)
    nk = qi + 1
    kf = float(topk)
    lane = lax.broadcasted_iota(I32, (tq, LANES), 1)
    krow = lax.broadcasted_iota(I32, (tq, tq), 0)
    qcol = lax.broadcasted_iota(I32, (tq, tq), 1)
    iwt = iwt_ref[0]

    for c in range(IDX_HEADS // 2):
        ch = iq_ref[0, :, c * LANES:(c + 1) * LANES]
        iqm_ref[2 * c] = jnp.where(lane < 64, ch, jnp.zeros_like(ch))
        iqm_ref[2 * c + 1] = jnp.where(lane >= 64, ch, jnp.zeros_like(ch))

    def idx_chunk(kc, masked):
        rows = pl.ds(pl.multiple_of(kc * tq, tq), tq)
        ik = ikd_ref[0, rows, :]
        tot = jnp.zeros((tq, tq), F32)
        for h in range(IDX_HEADS):
            tot = tot + jnp.maximum(_dot_nt(ik, iqm_ref[h]), 0.0) * iwt[h:h + 1, :]
        if masked:
            tot = jnp.where(krow <= qcol, tot, NEG_INF)
        key_ref[kc] = _sortable_key(tot)

    def idx_body(kc, carry):
        idx_chunk(kc, False)
        return carry

    lax.fori_loop(0, qi, idx_body, 0)
    idx_chunk(qi, True)

    def count(pred_fn):
        def body(kc, acc):
            f = jnp.where(pred_fn(key_ref[kc], kc), 1.0, 0.0)
            parts = [jnp.sum(f[r * 64:(r + 1) * 64], axis=0, keepdims=True) for r in range(tq // 64)]
            while len(parts) > 1:
                parts = [a + b for a, b in zip(parts[0::2], parts[1::2])]
            return acc + parts[0]
        return lax.fori_loop(0, nk, body, jnp.zeros((1, tq), F32))

    c0 = count(lambda blk, kc: blk >= 0)
    t = jnp.where(c0 >= kf, jnp.int32(0), jnp.int32(INT_MIN))

    def bit_body(i, t):
        cand = t | jnp.left_shift(jnp.int32(1), 30 - i)
        c = count(lambda blk, kc: blk >= cand)
        return jnp.where(c >= kf, cand, t)

    t = lax.fori_loop(0, 31, bit_body, t)

    cnt_gt = count(lambda blk, kc: blk > t)
    cnt_ge = count(lambda blk, kc: blk >= t)
    need = kf - cnt_gt
    excess = jnp.where((cnt_ge > kf) & (t > KEY_NEG_INF), 1.0, 0.0)
    any_excess = jnp.max(excess) > 0.0

    def tie_limit():
        def jbody(i, j):
            cand = j | jnp.left_shift(jnp.int32(1), idx_bits - 1 - i)
            c = count(lambda blk, kc: (blk == t) & ((kc * tq + krow) < cand))
            return jnp.where(c <= need, cand, j)
        return lax.fori_loop(0, idx_bits, jbody, jnp.zeros((1, tq), I32))

    jlim = lax.cond(any_excess, tie_limit, lambda: jnp.full((1, tq), 2 ** idx_bits - 1, I32))

    m_ref[...] = jnp.full(m_ref.shape, NEG_INF, F32)
    l_ref[...] = jnp.zeros(l_ref.shape, F32)
    acc_ref[...] = jnp.zeros(acc_ref.shape, F32)

    def att_chunk(kc, masked):
        rows = pl.ds(pl.multiple_of(kc * tq, tq), tq)
        blk = key_ref[kc]
        sel = (blk > t) | ((blk == t) & ((kc * tq + krow) < jlim))
        if masked:
            sel = sel & (krow <= qcol)
        bias = jnp.where(sel, 0.0, NEG_INF)
        for h in range(A_HEADS):
            kv = h // 4
            s_ref[h] = _dot_nt(k_ref[0, rows, kv * LANES:(kv + 1) * LANES],
                               qarr_ref[0, :, h * LANES:(h + 1) * LANES]) + bias
        for h in range(A_HEADS):
            _online_softmax_step(s_ref, vt_ref[0, kc, h // 4], h, m_ref, l_ref, acc_ref, guard=True)

    def att_body(kc, carry):
        att_chunk(kc, False)
        return carry

    lax.fori_loop(0, qi, att_body, 0)
    att_chunk(qi, True)
    _finish_heads(o_ref, l_ref, acc_ref, [(O_PERM[2 * i], O_PERM[2 * i + 1]) for i in range(A_HEADS // 2)])


def _dsa_prompt(q_arr, iq, iwt, k_bf, vt, ikd, tq):
    bsz, s, _ = q_arr.shape
    nk = s // tq
    topk = min(IDX_TOPK, s // 4)
    idx_bits = max(1, int(math.ceil(math.log2(s + 1))))
    qspec = lambda w: pl.BlockSpec((1, tq, w), lambda b, i: (b, i, 0))
    full = lambda w: pl.BlockSpec((1, s, w), lambda b, i: (b, 0, 0))
    return pl.pallas_call(
        functools.partial(_dsa_prompt_kernel, topk=topk, idx_bits=idx_bits),
        grid=(bsz, nk),
        in_specs=[qspec(1024), qspec(512), pl.BlockSpec((1, IDX_HEADS, tq), lambda b, i: (b, 0, i)),
                  full(256), pl.BlockSpec((1, nk, 2, VT_ROWS, tq), lambda b, i: (b, 0, 0, 0, 0)), full(LANES)],
        out_specs=qspec(512),
        out_shape=jax.ShapeDtypeStruct((bsz, s, 512), F32),
        scratch_shapes=[pltpu.VMEM((nk, tq, tq), I32),
                        pltpu.VMEM((IDX_HEADS, tq, LANES), BF16),
                        pltpu.VMEM((A_HEADS, tq, tq), F32),
                        pltpu.VMEM((A_HEADS, 1, tq), F32), pltpu.VMEM((A_HEADS, 1, tq), F32),
                        pltpu.VMEM((A_HEADS, LANES, tq), F32)],
        compiler_params=pltpu.CompilerParams(dimension_semantics=("arbitrary", "arbitrary"),
                                             vmem_limit_bytes=VMEM_LIMIT),
        name="dsa_prompt",
    )(q_arr, iq, iwt, k_bf, vt, ikd)


def _transposed_v_chunks(v, bsz, seq, tk):
    n = v.shape[1] // LANES
    vt = jnp.transpose(v.reshape(bsz, seq // tk, tk, n, LANES), (0, 1, 3, 4, 2))
    ones = jnp.ones((bsz, seq // tk, n, V_ONES_ROWS, tk), v.dtype)
    return jnp.concatenate([vt, ones], axis=3)


def _idx_scores_kernel(pt_ref, ikt_hbm, iq_ref, iknew_ref, iwcol_ref, sc_ref, snew_ref, buf, sem):
    b = pl.program_id(0)
    nb = pl.num_programs(0)
    n_pages = buf.shape[2] // PAGE_SIZE
    slot = b % 2

    def page_copy(bb, p, sl):
        dst = buf.at[sl, :, pl.ds(pl.multiple_of(p * PAGE_SIZE, PAGE_SIZE), PAGE_SIZE)]
        return pltpu.make_async_copy(ikt_hbm.at[pt_ref[bb, p]], dst, sem.at[sl])

    def fetch(bb, sl):
        def body(p, carry):
            page_copy(bb, p, sl).start()
            return carry
        lax.fori_loop(0, n_pages, body, 0)

    @pl.when(b == 0)
    def _():
        fetch(0, 0)

    @pl.when(b + 1 < nb)
    def _():
        fetch(b + 1, 1 - slot)

    def wait_body(p, carry):
        page_copy(b, p, slot).wait()
        return carry

    lax.fori_loop(0, n_pages, wait_body, 0)

    iq = iq_ref[0]
    iw = iwcol_ref[0]
    s = _dot(iq.astype(BF16), buf[slot].astype(BF16))
    sc_ref[0] = jnp.sum(jnp.maximum(s, 0.0) * iw, axis=0, keepdims=True)
    s_new = jnp.sum(iq * iknew_ref[0], axis=1, keepdims=True)
    s_new = jnp.sum(jnp.maximum(s_new, 0.0) * iw, axis=0, keepdims=True)
    snew_ref[0] = jnp.broadcast_to(s_new, (1, LANES))


def _idx_scores(page_table, ikt, iq3, ik_new, iw_col):
    db, n_pages = page_table.shape
    n_past = n_pages * PAGE_SIZE
    grid_spec = pltpu.PrefetchScalarGridSpec(
        num_scalar_prefetch=1,
        grid=(db,),
        in_specs=[pl.BlockSpec(memory_space=pl.ANY),
                  pl.BlockSpec((1, IDX_HEADS, IDX_DIM), lambda b, *_: (b, 0, 0)),
                  pl.BlockSpec((1, 1, IDX_DIM), lambda b, *_: (b, 0, 0)),
                  pl.BlockSpec((1, IDX_HEADS, 1), lambda b, *_: (b, 0, 0))],
        out_specs=[pl.BlockSpec((1, 1, n_past), lambda b, *_: (b, 0, 0)),
                   pl.BlockSpec((1, 1, LANES), lambda b, *_: (b, 0, 0))],
        scratch_shapes=[pltpu.VMEM((2, IDX_DIM, n_past), F32), pltpu.SemaphoreType.DMA((2,))])
    return pl.pallas_call(
        _idx_scores_kernel,
        grid_spec=grid_spec,
        out_shape=[jax.ShapeDtypeStruct((db, 1, n_past), F32), jax.ShapeDtypeStruct((db, 1, LANES), F32)],
        compiler_params=pltpu.CompilerParams(dimension_semantics=("arbitrary",),
                                             vmem_limit_bytes=VMEM_LIMIT),
        name="idx_scores",
    )(page_table, ikt, iq3, ik_new, iw_col)


def _topk_sample_kernel(sc_ref, snew_ref, sel_ref, selnew_ref, key_ref, *, topk, idx_bits):
    n_pages, db, _ = sc_ref.shape
    n_past = n_pages * PAGE_SIZE
    kf = float(topk)

    def key_body(p, carry):
        key_ref[p] = _sortable_key(sc_ref[p])
        return carry

    lax.fori_loop(0, n_pages, key_body, 0)
    knew = _sortable_key(snew_ref[...])
    lane = lax.broadcasted_iota(I32, (db, LANES), 1)

    def count(pred_fn):
        def body(p, acc):
            return acc + jnp.where(pred_fn(key_ref[p], p * PAGE_SIZE + lane), 1.0, 0.0)
        acc = lax.fori_loop(0, n_pages, body, jnp.zeros((db, LANES), F32))
        tot = jnp.broadcast_to(jnp.sum(acc, axis=1, keepdims=True), (db, LANES))
        return tot + jnp.where(pred_fn(knew, n_past), 1.0, 0.0)

    c0 = count(lambda kk, ii: kk >= 0)
    t = jnp.where(c0 >= kf, jnp.int32(0), jnp.int32(INT_MIN))

    def bit_body(i, t):
        cand = t | jnp.left_shift(jnp.int32(1), 30 - i)
        return jnp.where(count(lambda kk, ii: kk >= cand) >= kf, cand, t)

    t = lax.fori_loop(0, 31, bit_body, t)
    need = kf - count(lambda kk, ii: kk > t)
    excess = jnp.where(count(lambda kk, ii: kk >= t) > kf, 1.0, 0.0)

    def tie_limit():
        def jbody(i, j):
            cand = j | jnp.left_shift(jnp.int32(1), idx_bits - 1 - i)
            c = count(lambda kk, ii: (kk == t) & (ii < cand))
            return jnp.where(c <= need, cand, j)
        return lax.fori_loop(0, idx_bits, jbody, jnp.zeros((db, LANES), I32))

    jlim = lax.cond(jnp.max(excess) > 0.0, tie_limit,
                    lambda: jnp.full((db, LANES), 2 ** idx_bits - 1, I32))

    def sel_body(p, carry):
        kk = key_ref[p]
        sel_ref[p] = jnp.where((kk > t) | ((kk == t) & ((p * PAGE_SIZE + lane) < jlim)), 1.0, 0.0)
        return carry

    lax.fori_loop(0, n_pages, sel_body, 0)
    selnew_ref[...] = jnp.where((knew > t) | ((knew == t) & (n_past < jlim)), 1.0, 0.0)


def _topk_sample(scores_t, s_new):
    n_pages, db, _ = scores_t.shape
    length = n_pages * PAGE_SIZE + 1
    topk = min(IDX_TOPK, length // 4)
    idx_bits = max(1, int(math.ceil(math.log2(length + 1))))
    return pl.pallas_call(
        functools.partial(_topk_sample_kernel, topk=topk, idx_bits=idx_bits),
        out_shape=[jax.ShapeDtypeStruct((n_pages, db, PAGE_SIZE), F32),
                   jax.ShapeDtypeStruct((db, LANES), F32)],
        scratch_shapes=[pltpu.VMEM((n_pages, db, PAGE_SIZE), I32)],
        compiler_params=pltpu.CompilerParams(vmem_limit_bytes=VMEM_LIMIT),
        name="topk_sample",
    )(scores_t, s_new)


def _dsa_sample_kernel(pt_ref, kt_hbm, vt_hbm, qt_ref, q8_ref, sel_ref, selnew_ref, knew_ref, vnew_ref,
                       o_ref, kbuf, vbuf, qbc, s_ref, m_ref, l_ref, acc_ref, sem, *, ch):
    b = pl.program_id(0)
    c = pl.program_id(1)
    nb = pl.num_programs(0)
    nch = pl.num_programs(1)
    step = b * nch + c
    slot = step % 2

    def copies(bb, cc, sl, j):
        page = pt_ref[bb, cc * ch + j]
        return (pltpu.make_async_copy(kt_hbm.at[page], kbuf.at[sl, j], sem.at[0, sl]),
                pltpu.make_async_copy(vt_hbm.at[page], vbuf.at[sl, j], sem.at[1, sl]))

    def fetch(bb, cc, sl):
        for j in range(ch):
            ck, cv = copies(bb, cc, sl, j)
            ck.start()
            cv.start()

    @pl.when(step == 0)
    def _():
        fetch(0, 0, 0)

    @pl.when(step + 1 < nb * nch)
    def _():
        nxt = step + 1
        fetch(nxt // nch, nxt % nch, 1 - slot)

    @pl.when(c == 0)
    def _():
        m_ref[...] = jnp.full(m_ref.shape, NEG_INF, F32)
        l_ref[...] = jnp.zeros(l_ref.shape, F32)
        acc_ref[...] = jnp.zeros(acc_ref.shape, F32)
        for h in range(A_HEADS):
            qbc[h] = jnp.broadcast_to(qt_ref[0, h], (A_HEAD_DIM, PAGE_SIZE))

    for j in range(ch):
        ck, cv = copies(b, c, slot, j)
        ck.wait()
        cv.wait()

    for j in range(ch):
        bias = jnp.where(sel_ref[0, j:j + 1, :] > 0.5, 0.0, NEG_INF)
        for h in range(A_HEADS):
            s = jnp.sum(kbuf[slot, j, h // 2] * qbc[h], axis=0, keepdims=True)
            s_ref[j, h:h + 1, :] = s + bias
    s_all = s_ref[...]
    mc = jnp.max(jnp.max(s_all, axis=0), axis=1, keepdims=True)
    m_old = m_ref[...]
    m_new = jnp.maximum(m_old, mc)
    m_safe = jnp.where(m_new == NEG_INF, 0.0, m_new)
    alpha = jnp.exp(m_old - m_safe)
    m_ref[...] = m_new
    p_all = jnp.exp(s_all - m_safe[None])
    l_ref[...] = alpha * l_ref[...] + jnp.sum(jnp.sum(p_all, axis=0), axis=1, keepdims=True)
    s_ref[...] = p_all
    for h in range(A_HEADS):
        a = acc_ref[h] * alpha[h:h + 1, :]
        for j in range(ch):
            a = a + vbuf[slot, j, h // 2] * s_ref[j, h:h + 1, :]
        acc_ref[h] = a

    @pl.when(c == nch - 1)
    def _():
        m_old = m_ref[...]
        s_new = jnp.sum(q8_ref[0] * knew_ref[0], axis=1, keepdims=True)
        s_new = jnp.where(selnew_ref[0, :, 0:1] > 0.5, s_new, NEG_INF)
        m_new = jnp.maximum(m_old, s_new)
        alpha = jnp.exp(m_old - m_new)
        p_new = jnp.exp(s_new - m_new)
        denom = alpha * l_ref[...] + p_new
        for h in range(A_HEADS):
            num = (jnp.sum(acc_ref[h], axis=1, keepdims=True) * alpha[h:h + 1, :]
                   + vnew_ref[0, h] * p_new[h:h + 1, :])
            o_ref[0, h] = num / denom[h:h + 1, :]


def _dsa_sample(page_table, kt, vt, qt, q8, sel, sel_new, knew8, vnew8t, ch):
    db, n_pages = page_table.shape
    nch = n_pages // ch
    hd = A_HEAD_DIM
    grid_spec = pltpu.PrefetchScalarGridSpec(
        num_scalar_prefetch=1,
        grid=(db, nch),
        in_specs=[pl.BlockSpec(memory_space=pl.ANY), pl.BlockSpec(memory_space=pl.ANY),
                  pl.BlockSpec((1, A_HEADS, hd, 1), lambda b, c, *_: (b, 0, 0, 0)),
                  pl.BlockSpec((1, A_HEADS, hd), lambda b, c, *_: (b, 0, 0)),
                  pl.BlockSpec((1, ch, PAGE_SIZE), lambda b, c, *_: (b, c, 0)),
                  pl.BlockSpec((1, 1, LANES), lambda b, c, *_: (b, 0, 0)),
                  pl.BlockSpec((1, A_HEADS, hd), lambda b, c, *_: (b, 0, 0)),
                  pl.BlockSpec((1, A_HEADS, hd, 1), lambda b, c, *_: (b, 0, 0, 0))],
        out_specs=pl.BlockSpec((1, A_HEADS, hd, 1), lambda b, c, *_: (b, 0, 0, 0)),
        scratch_shapes=[pltpu.VMEM((2, ch, A_KV_HEADS, hd, PAGE_SIZE), F32),
                        pltpu.VMEM((2, ch, A_KV_HEADS, hd, PAGE_SIZE), F32),
                        pltpu.VMEM((A_HEADS, hd, PAGE_SIZE), F32),
                        pltpu.VMEM((ch, A_HEADS, PAGE_SIZE), F32),
                        pltpu.VMEM((A_HEADS, 1), F32), pltpu.VMEM((A_HEADS, 1), F32),
                        pltpu.VMEM((A_HEADS, hd, PAGE_SIZE), F32),
                        pltpu.SemaphoreType.DMA((2, 2))])
    return pl.pallas_call(
        functools.partial(_dsa_sample_kernel, ch=ch),
        grid_spec=grid_spec,
        out_shape=jax.ShapeDtypeStruct((db, A_HEADS, hd, 1), F32),
        compiler_params=pltpu.CompilerParams(dimension_semantics=("arbitrary", "arbitrary"),
                                             vmem_limit_bytes=VMEM_LIMIT),
        name="dsa_sample",
    )(page_table, kt, vt, qt, q8, sel, sel_new, knew8, vnew8t)


def _qabs_kernel(qb_ref, wukt_ref, o_ref):
    for h in range(B_HEADS):
        o_ref[h] = _dot(qb_ref[:, h * LANES:(h + 1) * LANES], wukt_ref[h]).astype(BF16)


def _qabs(q_b, wukt):
    db = q_b.shape[0]
    return pl.pallas_call(
        _qabs_kernel,
        out_shape=jax.ShapeDtypeStruct((B_HEADS, db, B_KV_LORA), BF16),
        name="mla_qabs",
    )(q_b, wukt)


def _ouv_kernel(olat_ref, wuv_ref, o_ref):
    db = olat_ref.shape[1]
    lane = lax.broadcasted_iota(I32, (db, LANES), 1)
    for c in range(B_HEADS // 2):
        w = wuv_ref[:, c * LANES:(c + 1) * LANES]
        r0 = _dot(olat_ref[2 * c].astype(BF16), w)
        r1 = _dot(olat_ref[2 * c + 1].astype(BF16), w)
        o_ref[:, c * LANES:(c + 1) * LANES] = jnp.where(lane < 64, r0, r1)


def _ouv(o_lat, wuv):
    db = o_lat.shape[1]
    return pl.pallas_call(
        _ouv_kernel,
        out_shape=jax.ShapeDtypeStruct((db, B_HEADS * B_VDIM), F32),
        name="mla_ouv",
    )(o_lat, wuv)


def _mla_sample_kernel(pt_ref, ckv_hbm, krt_hbm, qabs_ref, qrt_ref, qr_ref, cnew_ref, krnew_ref,
                       o_ref, cbuf, rbuf, qrbc, sr_ref, m_ref, l_ref, acc_ref, sem, *, ch):
    b = pl.program_id(0)
    c = pl.program_id(1)
    nb = pl.num_programs(0)
    nch = pl.num_programs(1)
    step = b * nch + c
    slot = step % 2

    def copies(bb, cc, sl, j):
        page = pt_ref[bb, cc * ch + j]
        return (pltpu.make_async_copy(ckv_hbm.at[page], cbuf.at[sl, pl.ds(j * PAGE_SIZE, PAGE_SIZE)],
                                      sem.at[0, sl]),
                pltpu.make_async_copy(krt_hbm.at[page], rbuf.at[sl, j], sem.at[1, sl]))

    def fetch(bb, cc, sl):
        for j in range(ch):
            cc_, cr_ = copies(bb, cc, sl, j)
            cc_.start()
            cr_.start()

    @pl.when(step == 0)
    def _():
        fetch(0, 0, 0)

    @pl.when(step + 1 < nb * nch)
    def _():
        nxt = step + 1
        fetch(nxt // nch, nxt % nch, 1 - slot)

    @pl.when(c == 0)
    def _():
        m_ref[...] = jnp.full(m_ref.shape, NEG_INF, F32)
        l_ref[...] = jnp.zeros(l_ref.shape, F32)
        acc_ref[...] = jnp.zeros(acc_ref.shape, F32)
        for h in range(B_HEADS):
            qrbc[h] = jnp.broadcast_to(qrt_ref[0, h], (B_ROPE, PAGE_SIZE))

    for j in range(ch):
        cc_, cr_ = copies(b, c, slot, j)
        cc_.wait()
        cr_.wait()

    for j in range(ch):
        blk = rbuf[slot, j]
        for h in range(B_HEADS):
            sr_ref[h:h + 1, j * PAGE_SIZE:(j + 1) * PAGE_SIZE] = jnp.sum(blk * qrbc[h], axis=0, keepdims=True)
    cb = cbuf[slot].astype(BF16)
    s = _dot_nt(qabs_ref[0], cb) + sr_ref[...]
    m_old = m_ref[...]
    m_new = jnp.maximum(m_old, jnp.max(s, axis=1, keepdims=True))
    alpha = jnp.exp(m_old - m_new)
    p = jnp.exp(s - m_new)
    l_ref[...] = alpha * l_ref[...] + jnp.sum(p, axis=1, keepdims=True)
    acc_ref[...] = alpha * acc_ref[...] + _dot(p.astype(BF16), cb)
    m_ref[...] = m_new

    @pl.when(c == nch - 1)
    def _():
        m_old = m_ref[...]
        s_new = (jnp.sum(qabs_ref[0].astype(F32) * cnew_ref[0], axis=1, keepdims=True)
                 + jnp.sum(qr_ref[0] * krnew_ref[0], axis=1, keepdims=True))
        m_new = jnp.maximum(m_old, s_new)
        alpha = jnp.exp(m_old - m_new)
        p_new = jnp.exp(s_new - m_new)
        denom = alpha * l_ref[...] + p_new
        o_ref[0] = (alpha * acc_ref[...] + p_new * cnew_ref[0]) / denom


def _mla_sample(page_table, ckv_pages, krt, qabs, qrt, qr, c_new, kr_new, ch):
    db, n_pages = page_table.shape
    nch = n_pages // ch
    grid_spec = pltpu.PrefetchScalarGridSpec(
        num_scalar_prefetch=1,
        grid=(db, nch),
        in_specs=[pl.BlockSpec(memory_space=pl.ANY), pl.BlockSpec(memory_space=pl.ANY),
                  pl.BlockSpec((1, B_HEADS, B_KV_LORA), lambda b, c, *_: (b, 0, 0)),
                  pl.BlockSpec((1, B_HEADS, B_ROPE, 1), lambda b, c, *_: (b, 0, 0, 0)),
                  pl.BlockSpec((1, B_HEADS, B_ROPE), lambda b, c, *_: (b, 0, 0)),
                  pl.BlockSpec((1, 1, B_KV_LORA), lambda b, c, *_: (b, 0, 0)),
                  pl.BlockSpec((1, 1, B_ROPE), lambda b, c, *_: (b, 0, 0))],
        out_specs=pl.BlockSpec((1, B_HEADS, B_KV_LORA), lambda b, c, *_: (b, 0, 0)),
        scratch_shapes=[pltpu.VMEM((2, ch * PAGE_SIZE, B_KV_LORA), F32),
                        pltpu.VMEM((2, ch, B_ROPE, PAGE_SIZE), F32),
                        pltpu.VMEM((B_HEADS, B_ROPE, PAGE_SIZE), F32),
                        pltpu.VMEM((B_HEADS, ch * PAGE_SIZE), F32),
                        pltpu.VMEM((B_HEADS, 1), F32), pltpu.VMEM((B_HEADS, 1), F32),
                        pltpu.VMEM((B_HEADS, B_KV_LORA), F32),
                        pltpu.SemaphoreType.DMA((2, 2))])
    return pl.pallas_call(
        functools.partial(_mla_sample_kernel, ch=ch),
        grid_spec=grid_spec,
        out_shape=jax.ShapeDtypeStruct((db, B_HEADS, B_KV_LORA), F32),
        compiler_params=pltpu.CompilerParams(dimension_semantics=("arbitrary", "arbitrary"),
                                             vmem_limit_bytes=VMEM_LIMIT),
        name="mla_sample",
    )(page_table, ckv_pages, krt, qabs, qrt, qr, c_new, kr_new)


def _rope_tables(pos):
    pos = pos.astype(F32)[:, None]
    n = pos.shape[0]

    def cs(half):
        inv = jnp.power(ROPE_THETA, -jnp.arange(half, dtype=F32) / half)
        ang = pos * inv[None, :]
        return jnp.cos(ang), jnp.sin(ang)

    c, s = cs(A_HEAD_DIM // 2)
    c64 = jnp.tile(jnp.concatenate([c, c], axis=1), (1, 2))
    s64 = jnp.tile(jnp.concatenate([-s, s], axis=1), (1, 2))
    c, s = cs(B_ROPE // 2)
    one, zero = jnp.ones((n, B_NOPE), F32), jnp.zeros((n, B_NOPE), F32)
    cq = jnp.concatenate([one, c, c, one[:, :32]], axis=1)
    sq = jnp.concatenate([zero, -s, s, zero[:, :32]], axis=1)
    return c64, s64, cq, sq


def _pack_weights(norm_w, w_in, qn_a, kn_a, q_lora_norm, w_uq, qn_b, kv_lora_norm, kn_b, w_uk, w_uv,
                  w_pa, w_pb, w_o):
    d = w_in.shape[0]
    sizes = (512, 256, 256, 512, IDX_DIM, IDX_HEADS, 512, B_Q_LORA, B_KV_LORA, B_ROPE, 512, d, d)
    offs = [0]
    for n in sizes:
        offs.append(offs[-1] + n)
    (w_qa, w_ka, w_va, w_iq, w_ik, w_iw, w_za, w_cq, w_ckv, w_kr, w_zb, w_ga, w_gb) = [
        w_in[:, offs[i]:offs[i + 1]] for i in range(len(sizes))]
    zc = lambda n: jnp.zeros((d, n), F32)
    w_all = jnp.concatenate(
        [w_qa, w_ka, w_va, w_iq, w_ik, w_ik, w_iw, zc(LANES - IDX_HEADS), w_cq, w_ckv,
         zc(B_NOPE), w_kr, zc(LANES - B_NOPE - B_ROPE)], axis=1).astype(BF16)
    perm = jnp.array(O_PERM)
    w_za_p = w_za.reshape(d, A_HEADS, A_HEAD_DIM)[:, perm].reshape(d, 512)
    wg = jnp.concatenate([w_za_p, w_zb, w_ga, w_gb], axis=1).astype(BF16)
    wpa = w_pa.reshape(A_HEADS, A_HEAD_DIM, d)[perm].reshape(512, d).astype(BF16)
    wuq3 = w_uq.reshape(B_Q_LORA, B_HEADS, B_QK)
    wuq = jnp.pad(wuq3, ((0, 0), (0, 0), (0, LANES - B_QK))).reshape(B_Q_LORA, B_HEADS * LANES).astype(BF16)
    wuk = jnp.pad(w_uk, ((0, 0), (0, 0), (0, LANES - B_NOPE))).reshape(B_KV_LORA, B_HEADS * LANES).astype(BF16)
    wukt = jnp.pad(jnp.transpose(w_uk, (1, 2, 0)), ((0, 0), (0, LANES - B_NOPE), (0, 0))).astype(BF16)
    wuv = w_uv.reshape(B_KV_LORA, B_HEADS * B_VDIM).astype(BF16)
    z32 = jnp.zeros((32,), F32)
    r = jnp.arange(LANES)
    return dict(
        norm_w=norm_w.reshape(1, d), w_all=w_all, wg=wg, wpa=wpa, wpb=w_pb.astype(BF16), wo=w_o.astype(BF16),
        gq=jnp.tile(qn_a, 2).reshape(1, LANES), gk=jnp.tile(kn_a, 2).reshape(1, LANES),
        gcq=q_lora_norm.reshape(1, B_Q_LORA), gckv=kv_lora_norm.reshape(1, B_KV_LORA),
        gqb=jnp.concatenate([qn_b, z32]).reshape(1, LANES),
        gkr=jnp.concatenate([jnp.zeros((B_NOPE,), F32), kn_b, z32]).reshape(1, LANES),
        wuq=wuq, wuk=wuk, wukt=wukt, wuv=wuv,
        bd64=(r[:, None] // 64 == r[None, :] // 64).astype(BF16),
        ones=jnp.ones((LANES, LANES), BF16))


def _tiles(seq, n_pages):
    return dict(tq=min(256, seq), dsa_pages=min(32, n_pages), mla_pages=min(32, n_pages))


def _layer(x_p, x_s, cache_k, cache_v, cache_ik, cache_ckv, cache_kr, page_table, wts):
    bsz, seq, d = x_p.shape
    db, dec_seq, _ = x_s.shape
    assert dec_seq == 1, "the sample group decodes one token per sequence"
    n_pages = page_table.shape[1]
    n_past = n_pages * PAGE_SIZE
    tiles = _tiles(seq, n_pages)
    tq = tiles["tq"]

    tabs_p = _rope_tables(jnp.arange(seq, dtype=I32))
    pr = _project(x_p.reshape(bsz * seq, d), tabs_p, seq // tq, wts, tq)
    r3 = lambda a: a.reshape(bsz, seq, a.shape[-1])
    iwt = jnp.swapaxes(r3(pr["iw_f"])[:, :, :IDX_HEADS], 1, 2)
    o_a = _dsa_prompt(r3(pr["q_arr"]), r3(pr["iq"]), iwt, r3(pr["k_bf"]),
                      _transposed_v_chunks(pr["v_bf"], bsz, seq, tq), r3(pr["ikd"]), tq)
    o_b = _mla_prompt(r3(pr["q_b"]), r3(pr["k_b"]), _transposed_v_chunks(pr["v_b"], bsz, seq, tq), tq)
    y_p = _merge(x_p.reshape(bsz * seq, d), o_a.reshape(bsz * seq, 512), o_b.reshape(bsz * seq, 512),
                 wts, tq).reshape(bsz, seq, d)

    tabs_s = _rope_tables(jnp.full((db,), n_past, dtype=I32))
    ps = _project(x_s.reshape(db, d), tabs_s, 1, wts, db)
    ikt = jnp.swapaxes(cache_ik, 1, 2)
    krt = jnp.swapaxes(cache_kr, 1, 2)
    kt = jnp.transpose(cache_k, (0, 2, 3, 1))
    vt = jnp.transpose(cache_v, (0, 2, 3, 1))
    iq3 = ps["iq"].astype(F32).reshape(db, IDX_HEADS, IDX_DIM)
    iw_col = ps["iw_f"][:, :IDX_HEADS, None]
    scores, s_new = _idx_scores(page_table, ikt, iq3, ps["ik_f"].reshape(db, 1, IDX_DIM), iw_col)
    scores_t = jnp.swapaxes(scores.reshape(db, n_pages, PAGE_SIZE), 0, 1)
    sel_t, sel_new = _topk_sample(scores_t, s_new.reshape(db, LANES))
    sel = jnp.swapaxes(sel_t, 0, 1)
    q4 = ps["q_arr"].astype(F32).reshape(db, A_HEADS, 2, 64)
    q8 = q4[:, :, 0] + q4[:, :, 1]
    knew8 = jnp.repeat(ps["k_f"].reshape(db, A_KV_HEADS, A_HEAD_DIM), A_HEADS // A_KV_HEADS, axis=1)
    vnew8 = jnp.repeat(ps["v_f"].reshape(db, A_KV_HEADS, A_HEAD_DIM), A_HEADS // A_KV_HEADS, axis=1)
    o_as = _dsa_sample(page_table, kt, vt, q8[..., None], q8, sel, sel_new.reshape(db, 1, LANES), knew8,
                       vnew8[..., None], ch=tiles["dsa_pages"])
    o_as = o_as.reshape(db, A_HEADS, A_HEAD_DIM)[:, jnp.array(O_PERM)].reshape(db, 512)

    qabs = jnp.swapaxes(_qabs(ps["q_b"], wts["wukt"]), 0, 1)
    qr = ps["q_b"].astype(F32).reshape(db, B_HEADS, LANES)[:, :, B_NOPE:B_NOPE + B_ROPE]
    o_lat = _mla_sample(page_table, cache_ckv, krt, qabs, qr[..., None], qr,
                        ps["ckv_f"].reshape(db, 1, B_KV_LORA), ps["kr_f"].reshape(db, 1, B_ROPE),
                        ch=tiles["mla_pages"])
    o_bs = _ouv(jnp.swapaxes(o_lat, 0, 1), wts["wuv"])
    y_s = _merge(x_s.reshape(db, d), o_as, o_bs, wts, db).reshape(db, 1, d)

    new_p = (pr["k_f"].reshape(bsz, seq, A_KV_HEADS, A_HEAD_DIM), pr["v_f"].reshape(bsz, seq, A_KV_HEADS, A_HEAD_DIM),
             pr["ik_f"].reshape(bsz, seq, IDX_DIM), pr["ckv_f"].reshape(bsz, seq, B_KV_LORA),
             pr["kr_f"].reshape(bsz, seq, B_ROPE))
    new_s = (ps["k_f"].reshape(db, 1, A_KV_HEADS, A_HEAD_DIM), ps["v_f"].reshape(db, 1, A_KV_HEADS, A_HEAD_DIM),
             ps["ik_f"].reshape(db, 1, IDX_DIM), ps["ckv_f"].reshape(db, 1, B_KV_LORA),
             ps["kr_f"].reshape(db, 1, B_ROPE))
    return y_p, y_s, new_p, new_s


def kernel(x_prompt, x_sample, cache_k, cache_v, cache_idx_k, cache_ckv, cache_kr, page_table, norm_w, w_in, qn_a, kn_a, q_lora_norm, w_uq, qn_b, kv_lora_norm, kn_b, w_uk, w_uv, w_pa, w_pb, w_o):
    depth = w_in.shape[0]
    x_p, x_s = x_prompt, x_sample
    outs_p, outs_s = [], []
    for l in range(depth):
        wts = _pack_weights(norm_w[l], w_in[l], qn_a[l], kn_a[l], q_lora_norm[l], w_uq[l], qn_b[l],
                            kv_lora_norm[l], kn_b[l], w_uk[l], w_uv[l], w_pa[l], w_pb[l], w_o[l])
        x_p, x_s, new_p, new_s = _layer(x_p, x_s, cache_k[l], cache_v[l], cache_idx_k[l], cache_ckv[l],
                                        cache_kr[l], page_table, wts)
        outs_p.append(new_p)
        outs_s.append(new_s)
    stack = lambda outs, i: jnp.stack([o[i] for o in outs])
    return (x_p, x_s) + tuple(stack(outs_p, i) for i in range(5)) + tuple(stack(outs_s, i) for i in range(5))
```

```python
import functools
import math

import jax
import jax.numpy as jnp
from jax import lax
from jax.experimental import pallas as pl
from jax.experimental.pallas import tpu as pltpu

F32 = jnp.float32
BF16 = jnp.bfloat16
I32 = jnp.int32

A_HEADS = 8
A_KV_HEADS = 4
A_HEAD_DIM = 64
IDX_HEADS = 8
IDX_DIM = 64
IDX_TOPK = 256
B_HEADS = 8
B_NOPE = 64
B_ROPE = 32
B_QK = B_NOPE + B_ROPE
B_VDIM = 64
B_Q_LORA = 384
B_KV_LORA = 256
PAGE_SIZE = 128
ROPE_THETA = 10000.0
NORM_EPS = 1e-6
A_SCALE = A_HEAD_DIM ** -0.5
B_SCALE = B_QK ** -0.5
IDX_SCALE = IDX_DIM ** -0.5
IDX_W_SCALE = IDX_HEADS ** -0.5

LANES = 128
VMEM_LIMIT = 56 * 1024 * 1024
V_ONES_ROWS = 16
VT_ROWS = LANES + V_ONES_ROWS
LOG2E = math.log2(math.e)
NEG_INF = float("-inf")
INT_MIN = -2 ** 31
KEY_NEG_INF = -2 ** 31 + 0x7FFFFF
O_PERM = (0, 2, 1, 3, 4, 6, 5, 7)

_C_QA, _C_KA, _C_VA, _C_IQ, _C_IKD, _C_IW, _C_CQ, _C_CKV, _C_KR, _C_END = (
    0, 512, 768, 1024, 1536, 1664, 1792, 2176, 2432, 2560)


def _dot(a, b):
    return jnp.dot(a, b, preferred_element_type=F32)


def _dot_nt(a, b):
    return lax.dot_general(a, b, (((1,), (1,)), ((), ())), preferred_element_type=F32)


def _threshold_from_key(key):
    key = jnp.maximum(key, KEY_NEG_INF)
    bits = jnp.where(key < 0, key ^ jnp.int32(0x7FFFFFFF), key)
    return lax.bitcast_convert_type(bits, F32)


def _kth_largest_threshold(count_ge, shape, kth):
    t = jnp.where(count_ge(jnp.zeros(shape, F32)) >= kth, jnp.int32(0), jnp.int32(INT_MIN))

    def bit_body(i, t):
        cand = t | jnp.left_shift(jnp.int32(1), 30 - i)
        return jnp.where(count_ge(_threshold_from_key(cand)) >= kth, cand, t)

    return _threshold_from_key(lax.fori_loop(0, 31, bit_body, jnp.broadcast_to(t, shape)))


def _store_vt_chunks(ref, vt):
    tm = vt.shape[1]
    for c in range(vt.shape[0] // LANES):
        ref[0, c, :LANES, :] = vt[c * LANES:(c + 1) * LANES].astype(BF16)
        ref[0, c, LANES:, :] = jnp.ones((V_ONES_ROWS, tm), BF16)


def _proj_kernel(x_ref, c64_ref, s64_ref, cq_ref, sq_ref, nw_ref, w_ref, gq_ref, gk_ref,
                 gcq_ref, gckv_ref, gqb_ref, gkr_ref, wuq_ref, wuk_ref, wuv_ref, bd64_ref, ones_ref,
                 kt_ref, vt_ref, ikt_ref, ckvf_ref, krt_ref, iwf_ref,
                 qarr_ref, kbf_ref, vtc_ref, iq_ref, ikd_ref, qb_ref, kb_ref, vbtc_ref):
    tm = x_ref.shape[0]
    x = x_ref[...]
    ms = jnp.mean(x * x, axis=-1, keepdims=True)
    h = (x * lax.rsqrt(ms + NORM_EPS) * nw_ref[...]).astype(BF16)
    lane = lax.broadcasted_iota(I32, (tm, LANES), 1)
    lo_half = lane < 64
    c64, s64, cq, sq = c64_ref[...], s64_ref[...], cq_ref[...], sq_ref[...]
    bd64, ones = bd64_ref[...], ones_ref[...]

    def proj(a, b):
        return _dot(h, w_ref[:, a:b])

    def group_sum(s, m2):
        hi = s.astype(BF16)
        lo = (s - hi.astype(F32)).astype(BF16)
        return _dot(jnp.concatenate([hi, lo], axis=1), m2)

    def rope(y, cos, sin, half):
        first = (lane % (2 * half)) < half
        sw = jnp.where(first, pltpu.roll(y, LANES - half, 1), pltpu.roll(y, half, 1))
        return y * cos + sw * sin

    def head_norm(uc, m, inv_n, g):
        ss = group_sum(uc * uc, m)
        return uc * lax.rsqrt(ss * inv_n + NORM_EPS) * g

    u = proj(_C_QA, _C_KA)
    for c in range(4):
        y = head_norm(u[:, c * LANES:(c + 1) * LANES], bd64, 1.0 / A_HEAD_DIM, gq_ref[...])
        y = rope(y, c64, s64, A_HEAD_DIM // 2) * (A_SCALE * LOG2E)
        r = pltpu.roll(y, 64, 1)
        if c % 2 == 0:
            he, ho = jnp.where(lo_half, y, 0.0), jnp.where(lo_half, r, 0.0)
        else:
            he, ho = jnp.where(lo_half, 0.0, r), jnp.where(lo_half, 0.0, y)
        qarr_ref[:, (2 * c) * LANES:(2 * c + 1) * LANES] = he.astype(BF16)
        qarr_ref[:, (2 * c + 1) * LANES:(2 * c + 2) * LANES] = ho.astype(BF16)

    u = proj(_C_KA, _C_VA)
    for c in range(2):
        y = head_norm(u[:, c * LANES:(c + 1) * LANES], bd64, 1.0 / A_HEAD_DIM, gk_ref[...])
        y = rope(y, c64, s64, A_HEAD_DIM // 2)
        kt_ref[0, c * LANES:(c + 1) * LANES, :] = y.T
        kbf_ref[:, c * LANES:(c + 1) * LANES] = y.astype(BF16)

    ut = proj(_C_VA, _C_IQ).T
    vt_ref[0] = ut
    _store_vt_chunks(vtc_ref, ut)

    u = proj(_C_IQ, _C_IKD)
    for c in range(4):
        y = rope(u[:, c * LANES:(c + 1) * LANES], c64, s64, IDX_DIM // 2) * IDX_SCALE
        iq_ref[:, c * LANES:(c + 1) * LANES] = y.astype(BF16)

    y = rope(proj(_C_IKD, _C_IW), c64, s64, IDX_DIM // 2)
    ikt_ref[0] = y.T[:IDX_DIM]
    ikd_ref[...] = y.astype(BF16)

    iwf_ref[...] = proj(_C_IW, _C_CQ) * IDX_W_SCALE

    u = proj(_C_CQ, _C_CKV)
    ms = jnp.mean(u * u, axis=-1, keepdims=True)
    cqn = (u * lax.rsqrt(ms + NORM_EPS) * gcq_ref[...]).astype(BF16)
    u = _dot(cqn, wuq_ref[...])
    for c in range(B_HEADS):
        y = head_norm(u[:, c * LANES:(c + 1) * LANES], ones, 1.0 / B_QK, gqb_ref[...])
        y = rope(y, cq, sq, B_ROPE // 2) * (B_SCALE * LOG2E)
        qb_ref[:, c * LANES:(c + 1) * LANES] = y.astype(BF16)

    u = proj(_C_CKV, _C_KR)
    ms = jnp.mean(u * u, axis=-1, keepdims=True)
    ckvn = u * lax.rsqrt(ms + NORM_EPS) * gckv_ref[...]
    ckvf_ref[...] = ckvn
    cb = ckvn.astype(BF16)
    _store_vt_chunks(vbtc_ref, _dot(cb, wuv_ref[...]).T)
    kn = _dot(cb, wuk_ref[...])
    u = proj(_C_KR, _C_END)
    y = head_norm(u, ones, 1.0 / B_ROPE, gkr_ref[...])
    y = rope(y, cq, sq, B_ROPE // 2)
    krt_ref[0] = y.T[B_NOPE:B_NOPE + B_ROPE]
    for c in range(B_HEADS):
        kb_ref[:, c * LANES:(c + 1) * LANES] = (kn[:, c * LANES:(c + 1) * LANES] + y).astype(BF16)


def _const_spec(shape):
    nd = len(shape)
    return pl.BlockSpec(shape, lambda *_: (0,) * nd, pipeline_mode=pl.Buffered(1))


def _project(x3d, tables, wts, tm):
    nseq, seq, d = x3d.shape
    n = nseq * seq
    per_seq = seq // tm
    row = lambda w: pl.BlockSpec((tm, w), lambda i: (i, 0))
    tab = pl.BlockSpec((tm, LANES), lambda i: (i % per_seq, 0))
    tr = lambda w: pl.BlockSpec((1, w, tm), lambda i: (i // per_seq, 0, i % per_seq))
    chunks = lambda c: pl.BlockSpec((1, c, VT_ROWS, tm), lambda i: (i, 0, 0, 0))
    consts = [wts["norm_w"], wts["w_all"], wts["gq"], wts["gk"], wts["gcq"], wts["gckv"], wts["gqb"],
              wts["gkr"], wts["wuq"], wts["wuk"], wts["wuv"], wts["bd64"], wts["ones"]]
    rows = lambda w, t: (row(w), jax.ShapeDtypeStruct((n, w), t))
    trs = lambda w: (tr(w), jax.ShapeDtypeStruct((nseq, w, seq), F32))
    chs = lambda c: (chunks(c), jax.ShapeDtypeStruct((n // tm, c, VT_ROWS, tm), BF16))
    outs = dict(k_t=trs(256), v_t=trs(256), ik_t=trs(IDX_DIM), ckv_f=rows(B_KV_LORA, F32), kr_t=trs(B_ROPE),
                iw_f=rows(LANES, F32), q_arr=rows(1024, BF16), k_bf=rows(256, BF16), vtc=chs(2),
                iq=rows(512, BF16), ikd=rows(LANES, BF16), q_b=rows(1024, BF16), k_b=rows(1024, BF16),
                vbtc=chs(4))
    res = pl.pallas_call(
        _proj_kernel,
        grid=(n // tm,),
        in_specs=[row(d)] + [tab] * 4 + [_const_spec(c.shape) for c in consts],
        out_specs=[s for s, _ in outs.values()],
        out_shape=[t for _, t in outs.values()],
        compiler_params=pltpu.CompilerParams(dimension_semantics=("arbitrary",),
                                             vmem_limit_bytes=VMEM_LIMIT),
        name="proj",
    )(x3d.reshape(n, d), *tables, *consts)
    return dict(zip(outs.keys(), res))


def _sigmoid(z):
    return 1.0 / (1.0 + jnp.exp(-z))


def _merge_kernel(x_ref, oa_ref, ob_ref, nw_ref, wg_ref, wpa_ref, wpb_ref, wo_ref, y_ref):
    x = x_ref[...]
    ms = jnp.mean(x * x, axis=-1, keepdims=True)
    h = (x * lax.rsqrt(ms + NORM_EPS) * nw_ref[...]).astype(BF16)
    d = x.shape[1]
    za = _dot(h, wg_ref[:, 0:512])
    a = _dot((oa_ref[...] * (za * _sigmoid(za))).astype(BF16), wpa_ref[...])
    zb = _dot(h, wg_ref[:, 512:1024])
    b = _dot((ob_ref[...] * (zb * _sigmoid(zb))).astype(BF16), wpb_ref[...])
    m = _sigmoid(_dot(h, wg_ref[:, 1024:1024 + d])) * a
    m = m + _sigmoid(_dot(h, wg_ref[:, 1024 + d:1024 + 2 * d])) * b
    y_ref[...] = x + _dot(m.astype(BF16), wo_ref[...])


def _merge(x2d, o_a, o_b, wts, tm):
    n, d = x2d.shape
    row = lambda w: pl.BlockSpec((tm, w), lambda i: (i, 0))
    consts = [wts["norm_w"], wts["wg"], wts["wpa"], wts["wpb"], wts["wo"]]
    return pl.pallas_call(
        _merge_kernel,
        grid=(n // tm,),
        in_specs=[row(d), row(512), row(512)] + [_const_spec(c.shape) for c in consts],
        out_specs=row(d),
        out_shape=jax.ShapeDtypeStruct((n, d), F32),
        compiler_params=pltpu.CompilerParams(dimension_semantics=("arbitrary",),
                                             vmem_limit_bytes=VMEM_LIMIT),
        name="merge",
    )(x2d, o_a, o_b, *consts)


def _online_softmax_step(s_ref, vt, h, m_ref, l_ref, acc_ref, guard):
    s = s_ref[h]
    m_old = m_ref[h]
    m_new = jnp.maximum(m_old, jnp.max(s, axis=0, keepdims=True))
    m_sub = jnp.where(m_new == NEG_INF, 0.0, m_new) if guard else m_new
    p = jnp.exp2(s - m_sub)
    alpha = jnp.exp2(m_old - m_sub)
    pv = _dot(vt, p.astype(BF16))
    acc_ref[h] = alpha * acc_ref[h] + pv[:LANES]
    l_ref[h] = alpha * l_ref[h] + pv[LANES:LANES + 1]
    m_ref[h] = m_new


def _finish_heads(o_ref, l_ref, acc_ref, pairs):
    tq = acc_ref.shape[2]
    rowi = lax.broadcasted_iota(I32, (LANES, tq), 0)
    for i, (h0, h1) in enumerate(pairs):
        r0 = acc_ref[h0] * (1.0 / l_ref[h0])
        r1 = acc_ref[h1] * (1.0 / l_ref[h1])
        o_ref[0, :, i * LANES:(i + 1) * LANES] = jnp.where(rowi < 64, r0, r1).T


def _mla_prompt_kernel(qb_ref, kb_ref, vbt_ref, o_ref, s_ref, m_ref, l_ref, acc_ref):
    tq = qb_ref.shape[1]
    qi = pl.program_id(1)
    m_ref[...] = jnp.full(m_ref.shape, NEG_INF, F32)
    l_ref[...] = jnp.zeros(l_ref.shape, F32)
    acc_ref[...] = jnp.zeros(acc_ref.shape, F32)
    krow = lax.broadcasted_iota(I32, (tq, tq), 0)
    qcol = lax.broadcasted_iota(I32, (tq, tq), 1)

    def step(kc, masked):
        rows = pl.ds(pl.multiple_of(kc * tq, tq), tq)
        for h in range(B_HEADS):
            s = _dot_nt(kb_ref[0, rows, h * LANES:(h + 1) * LANES], qb_ref[0, :, h * LANES:(h + 1) * LANES])
            if masked:
                s = jnp.where(krow <= qcol, s, NEG_INF)
            s_ref[h] = s
        for h in range(B_HEADS):
            _online_softmax_step(s_ref, vbt_ref[0, kc, h // 2], h, m_ref, l_ref, acc_ref, guard=False)

    def body(kc, carry):
        step(kc, False)
        return carry

    lax.fori_loop(0, qi, body, 0)
    step(qi, True)
    _finish_heads(o_ref, l_ref, acc_ref, [(2 * c, 2 * c + 1) for c in range(B_HEADS // 2)])


def _mla_prompt(q_b, k_b, vbt, tq):
    bsz, s, _ = q_b.shape
    nk = s // tq
    return pl.pallas_call(
        _mla_prompt_kernel,
        grid=(bsz, nk),
        in_specs=[pl.BlockSpec((1, tq, 1024), lambda b, i: (b, i, 0)),
                  pl.BlockSpec((1, s, 1024), lambda b, i: (b, 0, 0)),
                  pl.BlockSpec((1, nk, 4, VT_ROWS, tq), lambda b, i: (b, 0, 0, 0, 0))],
        out_specs=pl.BlockSpec((1, tq, 512), lambda b, i: (b, i, 0)),
        out_shape=jax.ShapeDtypeStruct((bsz, s, 512), F32),
        scratch_shapes=[pltpu.VMEM((B_HEADS, tq, tq), F32),
                        pltpu.VMEM((B_HEADS, 1, tq), F32), pltpu.VMEM((B_HEADS, 1, tq), F32),
                        pltpu.VMEM((B_HEADS, LANES, tq), F32)],
        compiler_params=pltpu.CompilerParams(dimension_semantics=("arbitrary", "arbitrary"),
                                             vmem_limit_bytes=VMEM_LIMIT),
        name="mla_prompt",
    )(q_b, k_b, vbt)


def _dsa_prompt_kernel(qarr_ref, iq_ref, iwt_ref, k_ref, vt_ref, ikd_ref, o_ref,
                       sc_ref, iqm_ref, s_ref, m_ref, l_ref, acc_ref, *, topk, idx_bits):
    tq = qarr_ref.shape[1]
    qi = pl.program_id(1)
    nk = qi + 1
    kf = float(topk)
    lane = lax.broadcasted_iota(I32, (tq, LANES), 1)
    krow = lax.broadcasted_iota(I32, (tq, tq), 0)
    qcol = lax.broadcasted_iota(I32, (tq, tq), 1)
    iwt = iwt_ref[0]

    for c in range(IDX_HEADS // 2):
        ch = iq_ref[0, :, c * LANES:(c + 1) * LANES]
        iqm_ref[2 * c] = jnp.where(lane < 64, ch, jnp.zeros_like(ch))
        iqm_ref[2 * c + 1] = jnp.where(lane >= 64, ch, jnp.zeros_like(ch))

    def idx_chunk(kc, masked):
        rows = pl.ds(pl.multiple_of(kc * tq, tq), tq)
        ik = ikd_ref[0, rows, :]
        tot = jnp.zeros((tq, tq), F32)
        for h in range(IDX_HEADS):
            tot = tot + jnp.maximum(_dot_nt(ik, iqm_ref[h]), 0.0) * iwt[h:h + 1, :]
        if masked:
            tot = jnp.where(krow <= qcol, tot, NEG_INF)
        sc_ref[kc] = tot

    def idx_body(kc, carry):
        idx_chunk(kc, False)
        return carry

    lax.fori_loop(0, qi, idx_body, 0)
    idx_chunk(qi, True)

    def count(pred_fn):
        def body(kc, acc):
            f = jnp.where(pred_fn(sc_ref[kc], kc), 1.0, 0.0)
            parts = [jnp.sum(f[r * 64:(r + 1) * 64], axis=0, keepdims=True) for r in range(tq // 64)]
            while len(parts) > 1:
                parts = [a + b for a, b in zip(parts[0::2], parts[1::2])]
            return acc + parts[0]
        return lax.fori_loop(0, nk, body, jnp.zeros((1, tq), F32))

    t = _kth_largest_threshold(lambda c: count(lambda blk, kc: blk >= c), (1, tq), kf)

    cnt_gt = count(lambda blk, kc: blk > t)
    cnt_ge = count(lambda blk, kc: blk >= t)
    need = kf - cnt_gt
    excess = jnp.where((cnt_ge > kf) & (t > NEG_INF), 1.0, 0.0)
    any_excess = jnp.max(excess) > 0.0

    def tie_limit():
        def jbody(i, j):
            cand = j | jnp.left_shift(jnp.int32(1), idx_bits - 1 - i)
            c = count(lambda blk, kc: (blk == t) & ((kc * tq + krow) < cand))
            return jnp.where(c <= need, cand, j)
        return lax.fori_loop(0, idx_bits, jbody, jnp.zeros((1, tq), I32))

    jlim = lax.cond(any_excess, tie_limit, lambda: jnp.full((1, tq), 2 ** idx_bits - 1, I32))

    m_ref[...] = jnp.full(m_ref.shape, NEG_INF, F32)
    l_ref[...] = jnp.zeros(l_ref.shape, F32)
    acc_ref[...] = jnp.zeros(acc_ref.shape, F32)

    def att_chunk(kc, masked):
        rows = pl.ds(pl.multiple_of(kc * tq, tq), tq)
        blk = sc_ref[kc]
        sel = (blk > t) | ((blk == t) & ((kc * tq + krow) < jlim))
        if masked:
            sel = sel & (krow <= qcol)
        bias = jnp.where(sel, 0.0, NEG_INF)
        for h in range(A_HEADS):
            kv = h // 4
            s_ref[h] = _dot_nt(k_ref[0, rows, kv * LANES:(kv + 1) * LANES],
                               qarr_ref[0, :, h * LANES:(h + 1) * LANES]) + bias
        for h in range(A_HEADS):
            _online_softmax_step(s_ref, vt_ref[0, kc, h // 4], h, m_ref, l_ref, acc_ref, guard=True)

    def att_body(kc, carry):
        att_chunk(kc, False)
        return carry

    lax.fori_loop(0, qi, att_body, 0)
    att_chunk(qi, True)
    _finish_heads(o_ref, l_ref, acc_ref, [(O_PERM[2 * i], O_PERM[2 * i + 1]) for i in range(A_HEADS // 2)])


def _dsa_prompt(q_arr, iq, iwt, k_bf, vt, ikd, tq):
    bsz, s, _ = q_arr.shape
    nk = s // tq
    topk = min(IDX_TOPK, s // 4)
    idx_bits = max(1, int(math.ceil(math.log2(s + 1))))
    qspec = lambda w: pl.BlockSpec((1, tq, w), lambda b, i: (b, i, 0))
    full = lambda w: pl.BlockSpec((1, s, w), lambda b, i: (b, 0, 0))
    return pl.pallas_call(
        functools.partial(_dsa_prompt_kernel, topk=topk, idx_bits=idx_bits),
        grid=(bsz, nk),
        in_specs=[qspec(1024), qspec(512), pl.BlockSpec((1, IDX_HEADS, tq), lambda b, i: (b, 0, i)),
                  full(256), pl.BlockSpec((1, nk, 2, VT_ROWS, tq), lambda b, i: (b, 0, 0, 0, 0)), full(LANES)],
        out_specs=qspec(512),
        out_shape=jax.ShapeDtypeStruct((bsz, s, 512), F32),
        scratch_shapes=[pltpu.VMEM((nk, tq, tq), F32),
                        pltpu.VMEM((IDX_HEADS, tq, LANES), BF16),
                        pltpu.VMEM((A_HEADS, tq, tq), F32),
                        pltpu.VMEM((A_HEADS, 1, tq), F32), pltpu.VMEM((A_HEADS, 1, tq), F32),
                        pltpu.VMEM((A_HEADS, LANES, tq), F32)],
        compiler_params=pltpu.CompilerParams(dimension_semantics=("arbitrary", "arbitrary"),
                                             vmem_limit_bytes=VMEM_LIMIT),
        name="dsa_prompt",
    )(q_arr, iq, iwt, k_bf, vt, ikd)


def _idx_scores_kernel(pt_ref, ikt_hbm, iq_ref, iknew_ref, iwcol_ref, sc_ref, snew_ref, buf, sem):
    b = pl.program_id(0)
    nb = pl.num_programs(0)
    n_pages = buf.shape[2] // PAGE_SIZE
    slot = b % 2

    def page_copy(bb, p, sl):
        dst = buf.at[sl, :, pl.ds(pl.multiple_of(p * PAGE_SIZE, PAGE_SIZE), PAGE_SIZE)]
        return pltpu.make_async_copy(ikt_hbm.at[pt_ref[bb, p]], dst, sem.at[sl])

    def fetch(bb, sl):
        def body(p, carry):
            page_copy(bb, p, sl).start()
            return carry
        lax.fori_loop(0, n_pages, body, 0)

    @pl.when(b == 0)
    def _():
        fetch(0, 0)

    @pl.when(b + 1 < nb)
    def _():
        fetch(b + 1, 1 - slot)

    def wait_body(p, carry):
        page_copy(b, p, slot).wait()
        return carry

    lax.fori_loop(0, n_pages, wait_body, 0)

    iq = iq_ref[0]
    iw = iwcol_ref[0]
    s = _dot(iq.astype(BF16), buf[slot].astype(BF16))
    sc_ref[0] = jnp.sum(jnp.maximum(s, 0.0) * iw, axis=0, keepdims=True)
    s_new = jnp.sum(iq * iknew_ref[0], axis=1, keepdims=True)
    s_new = jnp.sum(jnp.maximum(s_new, 0.0) * iw, axis=0, keepdims=True)
    snew_ref[0] = jnp.broadcast_to(s_new, (1, LANES))


def _idx_scores(page_table, ikt, iq3, ik_new, iw_col):
    db, n_pages = page_table.shape
    n_past = n_pages * PAGE_SIZE
    grid_spec = pltpu.PrefetchScalarGridSpec(
        num_scalar_prefetch=1,
        grid=(db,),
        in_specs=[pl.BlockSpec(memory_space=pl.ANY),
                  pl.BlockSpec((1, IDX_HEADS, IDX_DIM), lambda b, *_: (b, 0, 0)),
                  pl.BlockSpec((1, 1, IDX_DIM), lambda b, *_: (b, 0, 0)),
                  pl.BlockSpec((1, IDX_HEADS, 1), lambda b, *_: (b, 0, 0))],
        out_specs=[pl.BlockSpec((1, 1, n_past), lambda b, *_: (b, 0, 0)),
                   pl.BlockSpec((1, 1, LANES), lambda b, *_: (b, 0, 0))],
        scratch_shapes=[pltpu.VMEM((2, IDX_DIM, n_past), F32), pltpu.SemaphoreType.DMA((2,))])
    return pl.pallas_call(
        _idx_scores_kernel,
        grid_spec=grid_spec,
        out_shape=[jax.ShapeDtypeStruct((db, 1, n_past), F32), jax.ShapeDtypeStruct((db, 1, LANES), F32)],
        compiler_params=pltpu.CompilerParams(dimension_semantics=("arbitrary",),
                                             vmem_limit_bytes=VMEM_LIMIT),
        name="idx_scores",
    )(page_table, ikt, iq3, ik_new, iw_col)


def _topk_sample_kernel(sc_ref, snew_ref, sel_ref, selnew_ref, *, topk, idx_bits):
    n_pages, db, _ = sc_ref.shape
    n_past = n_pages * PAGE_SIZE
    kf = float(topk)
    knew = snew_ref[...]
    lane = lax.broadcasted_iota(I32, (db, LANES), 1)

    def count(pred_fn):
        def body(p, acc):
            return acc + jnp.where(pred_fn(sc_ref[p], p * PAGE_SIZE + lane), 1.0, 0.0)
        acc = lax.fori_loop(0, n_pages, body, jnp.zeros((db, LANES), F32))
        tot = jnp.broadcast_to(jnp.sum(acc, axis=1, keepdims=True), (db, LANES))
        return tot + jnp.where(pred_fn(knew, n_past), 1.0, 0.0)

    t = _kth_largest_threshold(lambda c: count(lambda kk, ii: kk >= c), (db, LANES), kf)
    need = kf - count(lambda kk, ii: kk > t)
    excess = jnp.where(count(lambda kk, ii: kk >= t) > kf, 1.0, 0.0)

    def tie_limit():
        def jbody(i, j):
            cand = j | jnp.left_shift(jnp.int32(1), idx_bits - 1 - i)
            c = count(lambda kk, ii: (kk == t) & (ii < cand))
            return jnp.where(c <= need, cand, j)
        return lax.fori_loop(0, idx_bits, jbody, jnp.zeros((db, LANES), I32))

    jlim = lax.cond(jnp.max(excess) > 0.0, tie_limit,
                    lambda: jnp.full((db, LANES), 2 ** idx_bits - 1, I32))

    def sel_body(p, carry):
        kk = sc_ref[p]
        sel_ref[p] = jnp.where((kk > t) | ((kk == t) & ((p * PAGE_SIZE + lane) < jlim)), 1.0, 0.0)
        return carry

    lax.fori_loop(0, n_pages, sel_body, 0)
    selnew_ref[...] = jnp.where((knew > t) | ((knew == t) & (n_past < jlim)), 1.0, 0.0)


def _topk_sample(scores_t, s_new):
    n_pages, db, _ = scores_t.shape
    length = n_pages * PAGE_SIZE + 1
    topk = min(IDX_TOPK, length // 4)
    idx_bits = max(1, int(math.ceil(math.log2(length + 1))))
    return pl.pallas_call(
        functools.partial(_topk_sample_kernel, topk=topk, idx_bits=idx_bits),
        out_shape=[jax.ShapeDtypeStruct((n_pages, db, PAGE_SIZE), F32),
                   jax.ShapeDtypeStruct((db, LANES), F32)],
        compiler_params=pltpu.CompilerParams(vmem_limit_bytes=VMEM_LIMIT),
        name="topk_sample",
    )(scores_t, s_new)


def _dsa_sample_kernel(pt_ref, kt_hbm, vt_hbm, qt_ref, q8_ref, sel_ref, selnew_ref, knew_ref, vnew_ref,
                       o_ref, kbuf, vbuf, qbc, s_ref, m_ref, l_ref, acc_ref, sem, *, ch):
    b = pl.program_id(0)
    c = pl.program_id(1)
    nb = pl.num_programs(0)
    nch = pl.num_programs(1)
    step = b * nch + c
    slot = step % 2

    def copies(bb, cc, sl, j):
        page = pt_ref[bb, cc * ch + j]
        return (pltpu.make_async_copy(kt_hbm.at[page], kbuf.at[sl, j], sem.at[0, sl]),
                pltpu.make_async_copy(vt_hbm.at[page], vbuf.at[sl, j], sem.at[1, sl]))

    def fetch(bb, cc, sl):
        for j in range(ch):
            ck, cv = copies(bb, cc, sl, j)
            ck.start()
            cv.start()

    @pl.when(step == 0)
    def _():
        fetch(0, 0, 0)

    @pl.when(step + 1 < nb * nch)
    def _():
        nxt = step + 1
        fetch(nxt // nch, nxt % nch, 1 - slot)

    @pl.when(c == 0)
    def _():
        m_ref[...] = jnp.full(m_ref.shape, NEG_INF, F32)
        l_ref[...] = jnp.zeros(l_ref.shape, F32)
        acc_ref[...] = jnp.zeros(acc_ref.shape, F32)
        for h in range(A_HEADS):
            qbc[h] = jnp.broadcast_to(qt_ref[0, h], (A_HEAD_DIM, PAGE_SIZE))

    for j in range(ch):
        ck, cv = copies(b, c, slot, j)
        ck.wait()
        cv.wait()

    for j in range(ch):
        bias = jnp.where(sel_ref[0, j:j + 1, :] > 0.5, 0.0, NEG_INF)
        for h in range(A_HEADS):
            s = jnp.sum(kbuf[slot, j, h // 2] * qbc[h], axis=0, keepdims=True)
            s_ref[j, h:h + 1, :] = s + bias
    s_all = s_ref[...]
    mc = jnp.max(jnp.max(s_all, axis=0), axis=1, keepdims=True)
    m_old = m_ref[...]
    m_new = jnp.maximum(m_old, mc)
    m_safe = jnp.where(m_new == NEG_INF, 0.0, m_new)
    alpha = jnp.exp2(m_old - m_safe)
    m_ref[...] = m_new
    p_all = jnp.exp2(s_all - m_safe[None])
    l_ref[...] = alpha * l_ref[...] + jnp.sum(jnp.sum(p_all, axis=0), axis=1, keepdims=True)
    s_ref[...] = p_all
    for h in range(A_HEADS):
        a = acc_ref[h] * alpha[h:h + 1, :]
        for j in range(ch):
            a = a + vbuf[slot, j, h // 2] * s_ref[j, h:h + 1, :]
        acc_ref[h] = a

    @pl.when(c == nch - 1)
    def _():
        m_old = m_ref[...]
        s_new = jnp.sum(q8_ref[0] * knew_ref[0], axis=1, keepdims=True)
        s_new = jnp.where(selnew_ref[0, :, 0:1] > 0.5, s_new, NEG_INF)
        m_new = jnp.maximum(m_old, s_new)
        alpha = jnp.exp2(m_old - m_new)
        p_new = jnp.exp2(s_new - m_new)
        denom = alpha * l_ref[...] + p_new
        for h in range(A_HEADS):
            num = (jnp.sum(acc_ref[h], axis=1, keepdims=True) * alpha[h:h + 1, :]
                   + vnew_ref[0, h] * p_new[h:h + 1, :])
            o_ref[0, h] = num / denom[h:h + 1, :]


def _dsa_sample(page_table, kt, vt, qt, q8, sel, sel_new, knew8, vnew8t, ch):
    db, n_pages = page_table.shape
    nch = n_pages // ch
    hd = A_HEAD_DIM
    grid_spec = pltpu.PrefetchScalarGridSpec(
        num_scalar_prefetch=1,
        grid=(db, nch),
        in_specs=[pl.BlockSpec(memory_space=pl.ANY), pl.BlockSpec(memory_space=pl.ANY),
                  pl.BlockSpec((1, A_HEADS, hd, 1), lambda b, c, *_: (b, 0, 0, 0)),
                  pl.BlockSpec((1, A_HEADS, hd), lambda b, c, *_: (b, 0, 0)),
                  pl.BlockSpec((1, ch, PAGE_SIZE), lambda b, c, *_: (b, c, 0)),
                  pl.BlockSpec((1, 1, LANES), lambda b, c, *_: (b, 0, 0)),
                  pl.BlockSpec((1, A_HEADS, hd), lambda b, c, *_: (b, 0, 0)),
                  pl.BlockSpec((1, A_HEADS, hd, 1), lambda b, c, *_: (b, 0, 0, 0))],
        out_specs=pl.BlockSpec((1, A_HEADS, hd, 1), lambda b, c, *_: (b, 0, 0, 0)),
        scratch_shapes=[pltpu.VMEM((2, ch, A_KV_HEADS, hd, PAGE_SIZE), F32),
                        pltpu.VMEM((2, ch, A_KV_HEADS, hd, PAGE_SIZE), F32),
                        pltpu.VMEM((A_HEADS, hd, PAGE_SIZE), F32),
                        pltpu.VMEM((ch, A_HEADS, PAGE_SIZE), F32),
                        pltpu.VMEM((A_HEADS, 1), F32), pltpu.VMEM((A_HEADS, 1), F32),
                        pltpu.VMEM((A_HEADS, hd, PAGE_SIZE), F32),
                        pltpu.SemaphoreType.DMA((2, 2))])
    return pl.pallas_call(
        functools.partial(_dsa_sample_kernel, ch=ch),
        grid_spec=grid_spec,
        out_shape=jax.ShapeDtypeStruct((db, A_HEADS, hd, 1), F32),
        compiler_params=pltpu.CompilerParams(dimension_semantics=("arbitrary", "arbitrary"),
                                             vmem_limit_bytes=VMEM_LIMIT),
        name="dsa_sample",
    )(page_table, kt, vt, qt, q8, sel, sel_new, knew8, vnew8t)


def _qabs_kernel(qb_ref, wukt_ref, o_ref):
    for h in range(B_HEADS):
        o_ref[h] = _dot(qb_ref[:, h * LANES:(h + 1) * LANES], wukt_ref[h]).astype(BF16)


def _qabs(q_b, wukt):
    db = q_b.shape[0]
    return pl.pallas_call(
        _qabs_kernel,
        out_shape=jax.ShapeDtypeStruct((B_HEADS, db, B_KV_LORA), BF16),
        name="mla_qabs",
    )(q_b, wukt)


def _ouv_kernel(olat_ref, wuv_ref, o_ref):
    db = olat_ref.shape[1]
    lane = lax.broadcasted_iota(I32, (db, LANES), 1)
    for c in range(B_HEADS // 2):
        w = wuv_ref[:, c * LANES:(c + 1) * LANES]
        r0 = _dot(olat_ref[2 * c].astype(BF16), w)
        r1 = _dot(olat_ref[2 * c + 1].astype(BF16), w)
        o_ref[:, c * LANES:(c + 1) * LANES] = jnp.where(lane < 64, r0, r1)


def _ouv(o_lat, wuv):
    db = o_lat.shape[1]
    return pl.pallas_call(
        _ouv_kernel,
        out_shape=jax.ShapeDtypeStruct((db, B_HEADS * B_VDIM), F32),
        name="mla_ouv",
    )(o_lat, wuv)


def _mla_sample_kernel(pt_ref, ckv_hbm, krt_hbm, qabs_ref, qrt_ref, qr_ref, cnew_ref, krnew_ref,
                       o_ref, cbuf, rbuf, qrbc, sr_ref, m_ref, l_ref, acc_ref, sem, *, ch):
    b = pl.program_id(0)
    c = pl.program_id(1)
    nb = pl.num_programs(0)
    nch = pl.num_programs(1)
    step = b * nch + c
    slot = step % 2

    def copies(bb, cc, sl, j):
        page = pt_ref[bb, cc * ch + j]
        return (pltpu.make_async_copy(ckv_hbm.at[page], cbuf.at[sl, pl.ds(j * PAGE_SIZE, PAGE_SIZE)],
                                      sem.at[0, sl]),
                pltpu.make_async_copy(krt_hbm.at[page], rbuf.at[sl, j], sem.at[1, sl]))

    def fetch(bb, cc, sl):
        for j in range(ch):
            cc_, cr_ = copies(bb, cc, sl, j)
            cc_.start()
            cr_.start()

    @pl.when(step == 0)
    def _():
        fetch(0, 0, 0)

    @pl.when(step + 1 < nb * nch)
    def _():
        nxt = step + 1
        fetch(nxt // nch, nxt % nch, 1 - slot)

    @pl.when(c == 0)
    def _():
        m_ref[...] = jnp.full(m_ref.shape, NEG_INF, F32)
        l_ref[...] = jnp.zeros(l_ref.shape, F32)
        acc_ref[...] = jnp.zeros(acc_ref.shape, F32)
        for h in range(B_HEADS):
            qrbc[h] = jnp.broadcast_to(qrt_ref[0, h], (B_ROPE, PAGE_SIZE))

    for j in range(ch):
        cc_, cr_ = copies(b, c, slot, j)
        cc_.wait()
        cr_.wait()

    cb = cbuf[slot].astype(BF16)
    s_lat = _dot_nt(qabs_ref[0], cb)
    for j in range(ch):
        blk = rbuf[slot, j]
        for h in range(B_HEADS):
            sr_ref[h:h + 1, j * PAGE_SIZE:(j + 1) * PAGE_SIZE] = jnp.sum(blk * qrbc[h], axis=0, keepdims=True)
    s = s_lat + sr_ref[...]
    m_old = m_ref[...]
    m_new = jnp.maximum(m_old, jnp.max(s, axis=1, keepdims=True))
    alpha = jnp.exp2(m_old - m_new)
    p = jnp.exp2(s - m_new)
    l_ref[...] = alpha * l_ref[...] + jnp.sum(p, axis=1, keepdims=True)
    acc_ref[...] = alpha * acc_ref[...] + _dot(p.astype(BF16), cb)
    m_ref[...] = m_new

    @pl.when(c == nch - 1)
    def _():
        m_old = m_ref[...]
        s_new = (jnp.sum(qabs_ref[0].astype(F32) * cnew_ref[0], axis=1, keepdims=True)
                 + jnp.sum(qr_ref[0] * krnew_ref[0], axis=1, keepdims=True))
        m_new = jnp.maximum(m_old, s_new)
        alpha = jnp.exp2(m_old - m_new)
        p_new = jnp.exp2(s_new - m_new)
        denom = alpha * l_ref[...] + p_new
        o_ref[0] = (alpha * acc_ref[...] + p_new * cnew_ref[0]) / denom


def _mla_sample(page_table, ckv_pages, krt, qabs, qrt, qr, c_new, kr_new, ch):
    db, n_pages = page_table.shape
    nch = n_pages // ch
    grid_spec = pltpu.PrefetchScalarGridSpec(
        num_scalar_prefetch=1,
        grid=(db, nch),
        in_specs=[pl.BlockSpec(memory_space=pl.ANY), pl.BlockSpec(memory_space=pl.ANY),
                  pl.BlockSpec((1, B_HEADS, B_KV_LORA), lambda b, c, *_: (b, 0, 0)),
                  pl.BlockSpec((1, B_HEADS, B_ROPE, 1), lambda b, c, *_: (b, 0, 0, 0)),
                  pl.BlockSpec((1, B_HEADS, B_ROPE), lambda b, c, *_: (b, 0, 0)),
                  pl.BlockSpec((1, 1, B_KV_LORA), lambda b, c, *_: (b, 0, 0)),
                  pl.BlockSpec((1, 1, B_ROPE), lambda b, c, *_: (b, 0, 0))],
        out_specs=pl.BlockSpec((1, B_HEADS, B_KV_LORA), lambda b, c, *_: (b, 0, 0)),
        scratch_shapes=[pltpu.VMEM((2, ch * PAGE_SIZE, B_KV_LORA), F32),
                        pltpu.VMEM((2, ch, B_ROPE, PAGE_SIZE), F32),
                        pltpu.VMEM((B_HEADS, B_ROPE, PAGE_SIZE), F32),
                        pltpu.VMEM((B_HEADS, ch * PAGE_SIZE), F32),
                        pltpu.VMEM((B_HEADS, 1), F32), pltpu.VMEM((B_HEADS, 1), F32),
                        pltpu.VMEM((B_HEADS, B_KV_LORA), F32),
                        pltpu.SemaphoreType.DMA((2, 2))])
    return pl.pallas_call(
        functools.partial(_mla_sample_kernel, ch=ch),
        grid_spec=grid_spec,
        out_shape=jax.ShapeDtypeStruct((db, B_HEADS, B_KV_LORA), F32),
        compiler_params=pltpu.CompilerParams(dimension_semantics=("arbitrary", "arbitrary"),
                                             vmem_limit_bytes=VMEM_LIMIT),
        name="mla_sample",
    )(page_table, ckv_pages, krt, qabs, qrt, qr, c_new, kr_new)


def _rope_tables(pos):
    pos = pos.astype(F32)[:, None]
    n = pos.shape[0]

    def cs(half):
        inv = jnp.power(ROPE_THETA, -jnp.arange(half, dtype=F32) / half)
        ang = pos * inv[None, :]
        return jnp.cos(ang), jnp.sin(ang)

    c, s = cs(A_HEAD_DIM // 2)
    c64 = jnp.tile(jnp.concatenate([c, c], axis=1), (1, 2))
    s64 = jnp.tile(jnp.concatenate([-s, s], axis=1), (1, 2))
    c, s = cs(B_ROPE // 2)
    one, zero = jnp.ones((n, B_NOPE), F32), jnp.zeros((n, B_NOPE), F32)
    cq = jnp.concatenate([one, c, c, one[:, :32]], axis=1)
    sq = jnp.concatenate([zero, -s, s, zero[:, :32]], axis=1)
    return c64, s64, cq, sq


def _pack_weights(norm_w, w_in, qn_a, kn_a, q_lora_norm, w_uq, qn_b, kv_lora_norm, kn_b, w_uk, w_uv,
                  w_pa, w_pb, w_o):
    d = w_in.shape[0]
    sizes = (512, 256, 256, 512, IDX_DIM, IDX_HEADS, 512, B_Q_LORA, B_KV_LORA, B_ROPE, 512, d, d)
    offs = [0]
    for n in sizes:
        offs.append(offs[-1] + n)
    (w_qa, w_ka, w_va, w_iq, w_ik, w_iw, w_za, w_cq, w_ckv, w_kr, w_zb, w_ga, w_gb) = [
        w_in[:, offs[i]:offs[i + 1]] for i in range(len(sizes))]
    zc = lambda n: jnp.zeros((d, n), F32)
    w_all = jnp.concatenate(
        [w_qa, w_ka, w_va, w_iq, w_ik, w_ik, w_iw, zc(LANES - IDX_HEADS), w_cq, w_ckv,
         zc(B_NOPE), w_kr, zc(LANES - B_NOPE - B_ROPE)], axis=1).astype(BF16)
    perm = jnp.array(O_PERM)
    w_za_p = w_za.reshape(d, A_HEADS, A_HEAD_DIM)[:, perm].reshape(d, 512)
    wg = jnp.concatenate([w_za_p, w_zb, w_ga, w_gb], axis=1).astype(BF16)
    wpa = w_pa.reshape(A_HEADS, A_HEAD_DIM, d)[perm].reshape(512, d).astype(BF16)
    wuq3 = w_uq.reshape(B_Q_LORA, B_HEADS, B_QK)
    wuq = jnp.pad(wuq3, ((0, 0), (0, 0), (0, LANES - B_QK))).reshape(B_Q_LORA, B_HEADS * LANES).astype(BF16)
    wuk = jnp.pad(w_uk, ((0, 0), (0, 0), (0, LANES - B_NOPE))).reshape(B_KV_LORA, B_HEADS * LANES).astype(BF16)
    wukt = jnp.pad(jnp.transpose(w_uk, (1, 2, 0)), ((0, 0), (0, LANES - B_NOPE), (0, 0))).astype(BF16)
    wuv = w_uv.reshape(B_KV_LORA, B_HEADS * B_VDIM).astype(BF16)
    z32 = jnp.zeros((32,), F32)
    r = jnp.arange(LANES)
    return dict(
        norm_w=norm_w.reshape(1, d), w_all=w_all, wg=wg, wpa=wpa, wpb=w_pb.astype(BF16), wo=w_o.astype(BF16),
        gq=jnp.tile(qn_a, 2).reshape(1, LANES), gk=jnp.tile(kn_a, 2).reshape(1, LANES),
        gcq=q_lora_norm.reshape(1, B_Q_LORA), gckv=kv_lora_norm.reshape(1, B_KV_LORA),
        gqb=jnp.concatenate([qn_b, z32]).reshape(1, LANES),
        gkr=jnp.concatenate([jnp.zeros((B_NOPE,), F32), kn_b, z32]).reshape(1, LANES),
        wuq=wuq, wuk=wuk, wukt=wukt, wuv=wuv,
        bd64=jnp.tile((r[:, None] // 64 == r[None, :] // 64).astype(BF16), (2, 1)),
        ones=jnp.ones((2 * LANES, LANES), BF16))


def _tiles(seq, n_pages):
    return dict(tq=min(256, seq), dsa_pages=min(32, n_pages), mla_pages=min(32, n_pages))


def _layer(x_p, x_s, cache_k, cache_v, cache_ik, cache_ckv, cache_kr, page_table, wts):
    bsz, seq, d = x_p.shape
    db, dec_seq, _ = x_s.shape
    assert dec_seq == 1, "the sample group decodes one token per sequence"
    n_pages = page_table.shape[1]
    n_past = n_pages * PAGE_SIZE
    tiles = _tiles(seq, n_pages)
    tq = tiles["tq"]

    tabs_p = _rope_tables(jnp.arange(seq, dtype=I32))
    pr = _project(x_p, tabs_p, wts, tq)
    r3 = lambda a: a.reshape(bsz, seq, a.shape[-1])
    chunked = lambda a: a.reshape((bsz, seq // tq) + a.shape[1:])
    iwt = jnp.swapaxes(r3(pr["iw_f"])[:, :, :IDX_HEADS], 1, 2)
    o_a = _dsa_prompt(r3(pr["q_arr"]), r3(pr["iq"]), iwt, r3(pr["k_bf"]), chunked(pr["vtc"]), r3(pr["ikd"]), tq)
    o_b = _mla_prompt(r3(pr["q_b"]), r3(pr["k_b"]), chunked(pr["vbtc"]), tq)
    y_p = _merge(x_p.reshape(bsz * seq, d), o_a.reshape(bsz * seq, 512), o_b.reshape(bsz * seq, 512),
                 wts, tq).reshape(bsz, seq, d)

    tabs_s = _rope_tables(jnp.full((db,), n_past, dtype=I32))
    ps = _project(x_s.reshape(1, db, d), tabs_s, wts, db)
    for name in ("k", "v", "ik", "kr"):
        ps[name + "_f"] = ps[name + "_t"][0].T
    ikt = jnp.swapaxes(cache_ik, 1, 2)
    krt = jnp.swapaxes(cache_kr, 1, 2)
    kt = jnp.transpose(cache_k, (0, 2, 3, 1))
    vt = jnp.transpose(cache_v, (0, 2, 3, 1))
    iq3 = ps["iq"].astype(F32).reshape(db, IDX_HEADS, IDX_DIM)
    iw_col = ps["iw_f"][:, :IDX_HEADS, None]
    scores, s_new = _idx_scores(page_table, ikt, iq3, ps["ik_f"].reshape(db, 1, IDX_DIM), iw_col)
    scores_t = jnp.swapaxes(scores.reshape(db, n_pages, PAGE_SIZE), 0, 1)
    sel_t, sel_new = _topk_sample(scores_t, s_new.reshape(db, LANES))
    sel = jnp.swapaxes(sel_t, 0, 1)
    q4 = ps["q_arr"].astype(F32).reshape(db, A_HEADS, 2, 64)
    q8 = q4[:, :, 0] + q4[:, :, 1]
    knew8 = jnp.repeat(ps["k_f"].reshape(db, A_KV_HEADS, A_HEAD_DIM), A_HEADS // A_KV_HEADS, axis=1)
    vnew8 = jnp.repeat(ps["v_f"].reshape(db, A_KV_HEADS, A_HEAD_DIM), A_HEADS // A_KV_HEADS, axis=1)
    o_as = _dsa_sample(page_table, kt, vt, q8[..., None], q8, sel, sel_new.reshape(db, 1, LANES), knew8,
                       vnew8[..., None], ch=tiles["dsa_pages"])
    o_as = o_as.reshape(db, A_HEADS, A_HEAD_DIM)[:, jnp.array(O_PERM)].reshape(db, 512)

    qabs = jnp.swapaxes(_qabs(ps["q_b"], wts["wukt"]), 0, 1)
    qr = ps["q_b"].astype(F32).reshape(db, B_HEADS, LANES)[:, :, B_NOPE:B_NOPE + B_ROPE]
    o_lat = _mla_sample(page_table, cache_ckv, krt, qabs, qr[..., None], qr,
                        ps["ckv_f"].reshape(db, 1, B_KV_LORA), ps["kr_f"].reshape(db, 1, B_ROPE),
                        ch=tiles["mla_pages"])
    o_bs = _ouv(jnp.swapaxes(o_lat, 0, 1), wts["wuv"])
    y_s = _merge(x_s.reshape(db, d), o_as, o_bs, wts, db).reshape(db, 1, d)

    heads_t = lambda a: jnp.transpose(a.reshape(bsz, A_KV_HEADS, A_HEAD_DIM, seq), (0, 3, 1, 2))
    new_p = (heads_t(pr["k_t"]), heads_t(pr["v_t"]), jnp.swapaxes(pr["ik_t"], 1, 2),
             pr["ckv_f"].reshape(bsz, seq, B_KV_LORA), jnp.swapaxes(pr["kr_t"], 1, 2))
    new_s = (ps["k_f"].reshape(db, 1, A_KV_HEADS, A_HEAD_DIM), ps["v_f"].reshape(db, 1, A_KV_HEADS, A_HEAD_DIM),
             ps["ik_f"].reshape(db, 1, IDX_DIM), ps["ckv_f"].reshape(db, 1, B_KV_LORA),
             ps["kr_f"].reshape(db, 1, B_ROPE))
    return y_p, y_s, new_p, new_s


def kernel(x_prompt, x_sample, cache_k, cache_v, cache_idx_k, cache_ckv, cache_kr, page_table, norm_w, w_in, qn_a, kn_a, q_lora_norm, w_uq, qn_b, kv_lora_norm, kn_b, w_uk, w_uv, w_pa, w_pb, w_o):
    depth = w_in.shape[0]
    x_p, x_s = x_prompt, x_sample
    outs_p, outs_s = [], []
    for l in range(depth):
        wts = _pack_weights(norm_w[l], w_in[l], qn_a[l], kn_a[l], q_lora_norm[l], w_uq[l], qn_b[l],
                            kv_lora_norm[l], kn_b[l], w_uk[l], w_uv[l], w_pa[l], w_pb[l], w_o[l])
        x_p, x_s, new_p, new_s = _layer(x_p, x_s, cache_k[l], cache_v[l], cache_idx_k[l], cache_ckv[l],
                                        cache_kr[l], page_table, wts)
        outs_p.append(new_p)
        outs_s.append(new_s)
    stack = lambda outs, i: jnp.stack([o[i] for o in outs])
    return (x_p, x_s) + tuple(stack(outs_p, i) for i in range(5)) + tuple(stack(outs_s, i) for i in range(5))
```

```python
import functools
import math

import jax
import jax.numpy as jnp
from jax import lax
from jax.experimental import pallas as pl
from jax.experimental.pallas import tpu as pltpu

F32 = jnp.float32
BF16 = jnp.bfloat16
I32 = jnp.int32

A_HEADS = 8
A_KV_HEADS = 4
A_HEAD_DIM = 64
IDX_HEADS = 8
IDX_DIM = 64
IDX_TOPK = 256
B_HEADS = 8
B_NOPE = 64
B_ROPE = 32
B_QK = B_NOPE + B_ROPE
B_VDIM = 64
B_Q_LORA = 384
B_KV_LORA = 256
PAGE_SIZE = 128
ROPE_THETA = 10000.0
NORM_EPS = 1e-6
A_SCALE = A_HEAD_DIM ** -0.5
B_SCALE = B_QK ** -0.5
IDX_SCALE = IDX_DIM ** -0.5
IDX_W_SCALE = IDX_HEADS ** -0.5

LANES = 128
VMEM_LIMIT = 56 * 1024 * 1024
V_ONES_ROWS = 16
VT_ROWS = LANES + V_ONES_ROWS
LOG2E = math.log2(math.e)
NEG_INF = float("-inf")
INT_MIN = -2 ** 31
KEY_NEG_INF = -2 ** 31 + 0x7FFFFF
O_PERM = (0, 2, 1, 3, 4, 6, 5, 7)

_C_QA, _C_KA, _C_VA, _C_IQ, _C_IKD, _C_IW, _C_CQ, _C_CKV, _C_KR, _C_END = (
    0, 512, 768, 1024, 1536, 1664, 1792, 2176, 2432, 2560)


def _dot(a, b):
    return jnp.dot(a, b, preferred_element_type=F32)


def _dot_nt(a, b):
    return lax.dot_general(a, b, (((1,), (1,)), ((), ())), preferred_element_type=F32)


def _threshold_from_key(key):
    key = jnp.maximum(key, KEY_NEG_INF)
    bits = jnp.where(key < 0, key ^ jnp.int32(0x7FFFFFFF), key)
    return lax.bitcast_convert_type(bits, F32)


def _kth_largest_threshold(count_ge, shape, kth):
    t = jnp.where(count_ge(jnp.zeros(shape, F32)) >= kth, jnp.int32(0), jnp.int32(INT_MIN))

    def bit_body(i, t):
        cand = t | jnp.left_shift(jnp.int32(1), 30 - i)
        return jnp.where(count_ge(_threshold_from_key(cand)) >= kth, cand, t)

    return _threshold_from_key(lax.fori_loop(0, 31, bit_body, jnp.broadcast_to(t, shape)))


def _store_vt_chunks(ref, vt):
    tm = vt.shape[1]
    for c in range(vt.shape[0] // LANES):
        ref[0, c, :LANES, :] = vt[c * LANES:(c + 1) * LANES].astype(BF16)
        ref[0, c, LANES:, :] = jnp.ones((V_ONES_ROWS, tm), BF16)


def _proj_kernel(x_ref, c64_ref, s64_ref, cq_ref, sq_ref, nw_ref, w_ref, gq_ref, gk_ref,
                 gcq_ref, gckv_ref, gqb_ref, gkr_ref, wuq_ref, wuk_ref, wuv_ref, bd64_ref, ones_ref,
                 kt_ref, vt_ref, ikt_ref, ckvf_ref, krt_ref, iwf_ref,
                 qarr_ref, kbf_ref, vtc_ref, iq_ref, ikd_ref, qb_ref, kb_ref, vbtc_ref):
    tm = x_ref.shape[0]
    x = x_ref[...]
    ms = jnp.mean(x * x, axis=-1, keepdims=True)
    h = (x * lax.rsqrt(ms + NORM_EPS) * nw_ref[...]).astype(BF16)
    lane = lax.broadcasted_iota(I32, (tm, LANES), 1)
    lo_half = lane < 64
    c64, s64, cq, sq = c64_ref[...], s64_ref[...], cq_ref[...], sq_ref[...]
    bd64, ones = bd64_ref[...], ones_ref[...]

    u_all = _dot(h, w_ref[...])

    def proj(a, b):
        return u_all[:, a:b]

    def group_sum(s, m2):
        hi = s.astype(BF16)
        lo = (s - hi.astype(F32)).astype(BF16)
        return _dot(jnp.concatenate([hi, lo], axis=1), m2)

    def rope(y, cos, sin, half):
        first = (lane % (2 * half)) < half
        sw = jnp.where(first, pltpu.roll(y, LANES - half, 1), pltpu.roll(y, half, 1))
        return y * cos + sw * sin

    def head_norm(uc, m, inv_n, g):
        ss = group_sum(uc * uc, m)
        return uc * lax.rsqrt(ss * inv_n + NORM_EPS) * g

    u = proj(_C_QA, _C_KA)
    for c in range(4):
        y = head_norm(u[:, c * LANES:(c + 1) * LANES], bd64, 1.0 / A_HEAD_DIM, gq_ref[...])
        y = rope(y, c64, s64, A_HEAD_DIM // 2) * (A_SCALE * LOG2E)
        r = pltpu.roll(y, 64, 1)
        if c % 2 == 0:
            he, ho = jnp.where(lo_half, y, 0.0), jnp.where(lo_half, r, 0.0)
        else:
            he, ho = jnp.where(lo_half, 0.0, r), jnp.where(lo_half, 0.0, y)
        qarr_ref[:, (2 * c) * LANES:(2 * c + 1) * LANES] = he.astype(BF16)
        qarr_ref[:, (2 * c + 1) * LANES:(2 * c + 2) * LANES] = ho.astype(BF16)

    u = proj(_C_KA, _C_VA)
    for c in range(2):
        y = head_norm(u[:, c * LANES:(c + 1) * LANES], bd64, 1.0 / A_HEAD_DIM, gk_ref[...])
        y = rope(y, c64, s64, A_HEAD_DIM // 2)
        kt_ref[0, c * LANES:(c + 1) * LANES, :] = y.T
        kbf_ref[:, c * LANES:(c + 1) * LANES] = y.astype(BF16)

    ut = proj(_C_VA, _C_IQ).T
    vt_ref[0] = ut
    _store_vt_chunks(vtc_ref, ut)

    u = proj(_C_IQ, _C_IKD)
    for c in range(4):
        y = rope(u[:, c * LANES:(c + 1) * LANES], c64, s64, IDX_DIM // 2) * IDX_SCALE
        iq_ref[:, c * LANES:(c + 1) * LANES] = y.astype(BF16)

    y = rope(proj(_C_IKD, _C_IW), c64, s64, IDX_DIM // 2)
    ikt_ref[0] = y.T[:IDX_DIM]
    ikd_ref[...] = y.astype(BF16)

    iwf_ref[...] = proj(_C_IW, _C_CQ) * IDX_W_SCALE

    u = proj(_C_CQ, _C_CKV)
    ms = jnp.mean(u * u, axis=-1, keepdims=True)
    cqn = (u * lax.rsqrt(ms + NORM_EPS) * gcq_ref[...]).astype(BF16)
    u = _dot(cqn, wuq_ref[...])
    for c in range(B_HEADS):
        y = head_norm(u[:, c * LANES:(c + 1) * LANES], ones, 1.0 / B_QK, gqb_ref[...])
        y = rope(y, cq, sq, B_ROPE // 2) * (B_SCALE * LOG2E)
        qb_ref[:, c * LANES:(c + 1) * LANES] = y.astype(BF16)

    u = proj(_C_CKV, _C_KR)
    ms = jnp.mean(u * u, axis=-1, keepdims=True)
    ckvn = u * lax.rsqrt(ms + NORM_EPS) * gckv_ref[...]
    ckvf_ref[...] = ckvn
    cb = ckvn.astype(BF16)
    _store_vt_chunks(vbtc_ref, _dot(cb, wuv_ref[...]).T)
    kn = _dot(cb, wuk_ref[...])
    u = proj(_C_KR, _C_END)
    y = head_norm(u, ones, 1.0 / B_ROPE, gkr_ref[...])
    y = rope(y, cq, sq, B_ROPE // 2)
    krt_ref[0] = y.T[B_NOPE:B_NOPE + B_ROPE]
    for c in range(B_HEADS):
        kb_ref[:, c * LANES:(c + 1) * LANES] = (kn[:, c * LANES:(c + 1) * LANES] + y).astype(BF16)


def _const_spec(shape):
    nd = len(shape)
    return pl.BlockSpec(shape, lambda *_: (0,) * nd, pipeline_mode=pl.Buffered(1))


def _project(x3d, tables, wts, tm):
    nseq, seq, d = x3d.shape
    n = nseq * seq
    per_seq = seq // tm
    row = lambda w: pl.BlockSpec((tm, w), lambda i: (i, 0))
    tab = pl.BlockSpec((tm, LANES), lambda i: (i % per_seq, 0))
    tr = lambda w: pl.BlockSpec((1, w, tm), lambda i: (i // per_seq, 0, i % per_seq))
    chunks = lambda c: pl.BlockSpec((1, c, VT_ROWS, tm), lambda i: (i, 0, 0, 0))
    consts = [wts["norm_w"], wts["w_all"], wts["gq"], wts["gk"], wts["gcq"], wts["gckv"], wts["gqb"],
              wts["gkr"], wts["wuq"], wts["wuk"], wts["wuv"], wts["bd64"], wts["ones"]]
    rows = lambda w, t: (row(w), jax.ShapeDtypeStruct((n, w), t))
    trs = lambda w: (tr(w), jax.ShapeDtypeStruct((nseq, w, seq), F32))
    chs = lambda c: (chunks(c), jax.ShapeDtypeStruct((n // tm, c, VT_ROWS, tm), BF16))
    outs = dict(k_t=trs(256), v_t=trs(256), ik_t=trs(IDX_DIM), ckv_f=rows(B_KV_LORA, F32), kr_t=trs(B_ROPE),
                iw_f=rows(LANES, F32), q_arr=rows(1024, BF16), k_bf=rows(256, BF16), vtc=chs(2),
                iq=rows(512, BF16), ikd=rows(LANES, BF16), q_b=rows(1024, BF16), k_b=rows(1024, BF16),
                vbtc=chs(4))
    res = pl.pallas_call(
        _proj_kernel,
        grid=(n // tm,),
        in_specs=[row(d)] + [tab] * 4 + [_const_spec(c.shape) for c in consts],
        out_specs=[s for s, _ in outs.values()],
        out_shape=[t for _, t in outs.values()],
        compiler_params=pltpu.CompilerParams(dimension_semantics=("arbitrary",),
                                             vmem_limit_bytes=VMEM_LIMIT),
        name="proj",
    )(x3d.reshape(n, d), *tables, *consts)
    return dict(zip(outs.keys(), res))


def _sigmoid(z):
    return 1.0 / (1.0 + jnp.exp(-z))


def _merge_kernel(x_ref, oa_ref, ob_ref, nw_ref, wg_ref, wpa_ref, wpb_ref, wo_ref, y_ref):
    x = x_ref[...]
    ms = jnp.mean(x * x, axis=-1, keepdims=True)
    h = (x * lax.rsqrt(ms + NORM_EPS) * nw_ref[...]).astype(BF16)
    d = x.shape[1]
    g_all = _dot(h, wg_ref[...])
    za = g_all[:, 0:512]
    a = _dot((oa_ref[...] * (za * _sigmoid(za))).astype(BF16), wpa_ref[...])
    zb = g_all[:, 512:1024]
    b = _dot((ob_ref[...] * (zb * _sigmoid(zb))).astype(BF16), wpb_ref[...])
    m = _sigmoid(g_all[:, 1024:1024 + d]) * a
    m = m + _sigmoid(g_all[:, 1024 + d:1024 + 2 * d]) * b
    y_ref[...] = x + _dot(m.astype(BF16), wo_ref[...])


def _merge(x2d, o_a, o_b, wts, tm):
    n, d = x2d.shape
    row = lambda w: pl.BlockSpec((tm, w), lambda i: (i, 0))
    consts = [wts["norm_w"], wts["wg"], wts["wpa"], wts["wpb"], wts["wo"]]
    return pl.pallas_call(
        _merge_kernel,
        grid=(n // tm,),
        in_specs=[row(d), row(512), row(512)] + [_const_spec(c.shape) for c in consts],
        out_specs=row(d),
        out_shape=jax.ShapeDtypeStruct((n, d), F32),
        compiler_params=pltpu.CompilerParams(dimension_semantics=("arbitrary",),
                                             vmem_limit_bytes=VMEM_LIMIT),
        name="merge",
    )(x2d, o_a, o_b, *consts)


def _online_softmax_step(s_ref, vt, h, m_ref, l_ref, acc_ref, guard):
    s = s_ref[h]
    m_old = m_ref[h]
    m_new = jnp.maximum(m_old, jnp.max(s, axis=0, keepdims=True))
    m_sub = jnp.where(m_new == NEG_INF, 0.0, m_new) if guard else m_new
    p = jnp.exp2(s - m_sub)
    alpha = jnp.exp2(m_old - m_sub)
    pv = _dot(vt, p.astype(BF16))
    acc_ref[h] = alpha * acc_ref[h] + pv[:LANES]
    l_ref[h] = alpha * l_ref[h] + pv[LANES:LANES + 1]
    m_ref[h] = m_new


def _finish_heads(o_ref, l_ref, acc_ref, pairs):
    tq = acc_ref.shape[2]
    rowi = lax.broadcasted_iota(I32, (LANES, tq), 0)
    for i, (h0, h1) in enumerate(pairs):
        r0 = acc_ref[h0] * (1.0 / l_ref[h0])
        r1 = acc_ref[h1] * (1.0 / l_ref[h1])
        o_ref[0, :, i * LANES:(i + 1) * LANES] = jnp.where(rowi < 64, r0, r1).T


def _mla_prompt_kernel(qb_ref, kb_ref, vbt_ref, o_ref, s_ref, m_ref, l_ref, acc_ref):
    tq = qb_ref.shape[1]
    qi = pl.program_id(1)
    m_ref[...] = jnp.full(m_ref.shape, NEG_INF, F32)
    l_ref[...] = jnp.zeros(l_ref.shape, F32)
    acc_ref[...] = jnp.zeros(acc_ref.shape, F32)
    krow = lax.broadcasted_iota(I32, (tq, tq), 0)
    qcol = lax.broadcasted_iota(I32, (tq, tq), 1)

    def step(kc, masked):
        rows = pl.ds(pl.multiple_of(kc * tq, tq), tq)
        for h in range(B_HEADS):
            s = _dot_nt(kb_ref[0, rows, h * LANES:(h + 1) * LANES], qb_ref[0, :, h * LANES:(h + 1) * LANES])
            if masked:
                s = jnp.where(krow <= qcol, s, NEG_INF)
            s_ref[h] = s
        for h in range(B_HEADS):
            _online_softmax_step(s_ref, vbt_ref[0, kc, h // 2], h, m_ref, l_ref, acc_ref, guard=False)

    def body(kc, carry):
        step(kc, False)
        return carry

    lax.fori_loop(0, qi, body, 0)
    step(qi, True)
    _finish_heads(o_ref, l_ref, acc_ref, [(2 * c, 2 * c + 1) for c in range(B_HEADS // 2)])


def _mla_prompt(q_b, k_b, vbt, tq):
    bsz, s, _ = q_b.shape
    nk = s // tq
    return pl.pallas_call(
        _mla_prompt_kernel,
        grid=(bsz, nk),
        in_specs=[pl.BlockSpec((1, tq, 1024), lambda b, i: (b, i, 0)),
                  pl.BlockSpec((1, s, 1024), lambda b, i: (b, 0, 0)),
                  pl.BlockSpec((1, nk, 4, VT_ROWS, tq), lambda b, i: (b, 0, 0, 0, 0))],
        out_specs=pl.BlockSpec((1, tq, 512), lambda b, i: (b, i, 0)),
        out_shape=jax.ShapeDtypeStruct((bsz, s, 512), F32),
        scratch_shapes=[pltpu.VMEM((B_HEADS, tq, tq), F32),
                        pltpu.VMEM((B_HEADS, 1, tq), F32), pltpu.VMEM((B_HEADS, 1, tq), F32),
                        pltpu.VMEM((B_HEADS, LANES, tq), F32)],
        compiler_params=pltpu.CompilerParams(dimension_semantics=("arbitrary", "arbitrary"),
                                             vmem_limit_bytes=VMEM_LIMIT),
        name="mla_prompt",
    )(q_b, k_b, vbt)


def _dsa_prompt_kernel(qarr_ref, iq_ref, iwt_ref, k_ref, vt_ref, ikd_ref, o_ref,
                       sc_ref, iqm_ref, s_ref, m_ref, l_ref, acc_ref, *, topk, idx_bits):
    tq = qarr_ref.shape[1]
    qi = pl.program_id(1)
    nk = qi + 1
    kf = float(topk)
    lane = lax.broadcasted_iota(I32, (tq, LANES), 1)
    krow = lax.broadcasted_iota(I32, (tq, tq), 0)
    qcol = lax.broadcasted_iota(I32, (tq, tq), 1)
    iwt = iwt_ref[0]

    for c in range(IDX_HEADS // 2):
        ch = iq_ref[0, :, c * LANES:(c + 1) * LANES]
        iqm_ref[2 * c] = jnp.where(lane < 64, ch, jnp.zeros_like(ch))
        iqm_ref[2 * c + 1] = jnp.where(lane >= 64, ch, jnp.zeros_like(ch))

    def idx_chunk(kc, masked):
        rows = pl.ds(pl.multiple_of(kc * tq, tq), tq)
        ik = ikd_ref[0, rows, :]
        tot = jnp.zeros((tq, tq), F32)
        for h in range(IDX_HEADS):
            tot = tot + jnp.maximum(_dot_nt(ik, iqm_ref[h]), 0.0) * iwt[h:h + 1, :]
        if masked:
            tot = jnp.where(krow <= qcol, tot, NEG_INF)
        sc_ref[kc] = tot

    def idx_body(kc, carry):
        idx_chunk(kc, False)
        return carry

    lax.fori_loop(0, qi, idx_body, 0)
    idx_chunk(qi, True)

    def count(pred_fn):
        def body(kc, acc):
            f = jnp.where(pred_fn(sc_ref[kc], kc), 1.0, 0.0)
            parts = [jnp.sum(f[r * 64:(r + 1) * 64], axis=0, keepdims=True) for r in range(tq // 64)]
            while len(parts) > 1:
                parts = [a + b for a, b in zip(parts[0::2], parts[1::2])]
            return acc + parts[0]
        return lax.fori_loop(0, nk, body, jnp.zeros((1, tq), F32))

    t = _kth_largest_threshold(lambda c: count(lambda blk, kc: blk >= c), (1, tq), kf)

    cnt_gt = count(lambda blk, kc: blk > t)
    cnt_ge = count(lambda blk, kc: blk >= t)
    need = kf - cnt_gt
    excess = jnp.where((cnt_ge > kf) & (t > NEG_INF), 1.0, 0.0)
    any_excess = jnp.max(excess) > 0.0

    def tie_limit():
        def jbody(i, j):
            cand = j | jnp.left_shift(jnp.int32(1), idx_bits - 1 - i)
            c = count(lambda blk, kc: (blk == t) & ((kc * tq + krow) < cand))
            return jnp.where(c <= need, cand, j)
        return lax.fori_loop(0, idx_bits, jbody, jnp.zeros((1, tq), I32))

    jlim = lax.cond(any_excess, tie_limit, lambda: jnp.full((1, tq), 2 ** idx_bits - 1, I32))

    m_ref[...] = jnp.full(m_ref.shape, NEG_INF, F32)
    l_ref[...] = jnp.zeros(l_ref.shape, F32)
    acc_ref[...] = jnp.zeros(acc_ref.shape, F32)

    def att_chunk(kc, masked):
        rows = pl.ds(pl.multiple_of(kc * tq, tq), tq)
        blk = sc_ref[kc]
        sel = (blk > t) | ((blk == t) & ((kc * tq + krow) < jlim))
        if masked:
            sel = sel & (krow <= qcol)
        bias = jnp.where(sel, 0.0, NEG_INF)
        for h in range(A_HEADS):
            kv = h // 4
            s_ref[h] = _dot_nt(k_ref[0, rows, kv * LANES:(kv + 1) * LANES],
                               qarr_ref[0, :, h * LANES:(h + 1) * LANES]) + bias
        for h in range(A_HEADS):
            _online_softmax_step(s_ref, vt_ref[0, kc, h // 4], h, m_ref, l_ref, acc_ref, guard=True)

    def att_body(kc, carry):
        att_chunk(kc, False)
        return carry

    lax.fori_loop(0, qi, att_body, 0)
    att_chunk(qi, True)
    _finish_heads(o_ref, l_ref, acc_ref, [(O_PERM[2 * i], O_PERM[2 * i + 1]) for i in range(A_HEADS // 2)])


def _dsa_prompt(q_arr, iq, iwt, k_bf, vt, ikd, tq):
    bsz, s, _ = q_arr.shape
    nk = s // tq
    topk = min(IDX_TOPK, s // 4)
    idx_bits = max(1, int(math.ceil(math.log2(s + 1))))
    qspec = lambda w: pl.BlockSpec((1, tq, w), lambda b, i: (b, i, 0))
    full = lambda w: pl.BlockSpec((1, s, w), lambda b, i: (b, 0, 0))
    return pl.pallas_call(
        functools.partial(_dsa_prompt_kernel, topk=topk, idx_bits=idx_bits),
        grid=(bsz, nk),
        in_specs=[qspec(1024), qspec(512), pl.BlockSpec((1, IDX_HEADS, tq), lambda b, i: (b, 0, i)),
                  full(256), pl.BlockSpec((1, nk, 2, VT_ROWS, tq), lambda b, i: (b, 0, 0, 0, 0)), full(LANES)],
        out_specs=qspec(512),
        out_shape=jax.ShapeDtypeStruct((bsz, s, 512), F32),
        scratch_shapes=[pltpu.VMEM((nk, tq, tq), F32),
                        pltpu.VMEM((IDX_HEADS, tq, LANES), BF16),
                        pltpu.VMEM((A_HEADS, tq, tq), F32),
                        pltpu.VMEM((A_HEADS, 1, tq), F32), pltpu.VMEM((A_HEADS, 1, tq), F32),
                        pltpu.VMEM((A_HEADS, LANES, tq), F32)],
        compiler_params=pltpu.CompilerParams(dimension_semantics=("arbitrary", "arbitrary"),
                                             vmem_limit_bytes=VMEM_LIMIT),
        name="dsa_prompt",
    )(q_arr, iq, iwt, k_bf, vt, ikd)


def _idx_scores_kernel(pt_ref, ikt_hbm, iq_ref, iknew_ref, iwcol_ref, sc_ref, snew_ref, buf, sem):
    b = pl.program_id(0)
    nb = pl.num_programs(0)
    n_pages = buf.shape[2] // PAGE_SIZE
    slot = b % 2

    def page_copy(bb, p, sl):
        dst = buf.at[sl, :, pl.ds(pl.multiple_of(p * PAGE_SIZE, PAGE_SIZE), PAGE_SIZE)]
        return pltpu.make_async_copy(ikt_hbm.at[pt_ref[bb, p]], dst, sem.at[sl])

    def fetch(bb, sl):
        def body(p, carry):
            page_copy(bb, p, sl).start()
            return carry
        lax.fori_loop(0, n_pages, body, 0)

    @pl.when(b == 0)
    def _():
        fetch(0, 0)

    @pl.when(b + 1 < nb)
    def _():
        fetch(b + 1, 1 - slot)

    def wait_body(p, carry):
        page_copy(b, p, slot).wait()
        return carry

    lax.fori_loop(0, n_pages, wait_body, 0)

    iq = iq_ref[0]
    iw = iwcol_ref[0]
    s = _dot(iq.astype(BF16), buf[slot].astype(BF16))
    sc_ref[0] = jnp.sum(jnp.maximum(s, 0.0) * iw, axis=0, keepdims=True)
    s_new = jnp.sum(iq * iknew_ref[0], axis=1, keepdims=True)
    s_new = jnp.sum(jnp.maximum(s_new, 0.0) * iw, axis=0, keepdims=True)
    snew_ref[0] = jnp.broadcast_to(s_new, (1, LANES))


def _idx_scores(page_table, ikt, iq3, ik_new, iw_col):
    db, n_pages = page_table.shape
    n_past = n_pages * PAGE_SIZE
    grid_spec = pltpu.PrefetchScalarGridSpec(
        num_scalar_prefetch=1,
        grid=(db,),
        in_specs=[pl.BlockSpec(memory_space=pl.ANY),
                  pl.BlockSpec((1, IDX_HEADS, IDX_DIM), lambda b, *_: (b, 0, 0)),
                  pl.BlockSpec((1, 1, IDX_DIM), lambda b, *_: (b, 0, 0)),
                  pl.BlockSpec((1, IDX_HEADS, 1), lambda b, *_: (b, 0, 0))],
        out_specs=[pl.BlockSpec((1, 1, n_past), lambda b, *_: (b, 0, 0)),
                   pl.BlockSpec((1, 1, LANES), lambda b, *_: (b, 0, 0))],
        scratch_shapes=[pltpu.VMEM((2, IDX_DIM, n_past), F32), pltpu.SemaphoreType.DMA((2,))])
    return pl.pallas_call(
        _idx_scores_kernel,
        grid_spec=grid_spec,
        out_shape=[jax.ShapeDtypeStruct((db, 1, n_past), F32), jax.ShapeDtypeStruct((db, 1, LANES), F32)],
        compiler_params=pltpu.CompilerParams(dimension_semantics=("arbitrary",),
                                             vmem_limit_bytes=VMEM_LIMIT),
        name="idx_scores",
    )(page_table, ikt, iq3, ik_new, iw_col)


def _topk_sample_kernel(sc_ref, snew_ref, sel_ref, selnew_ref, *, topk, idx_bits):
    n_pages, db, _ = sc_ref.shape
    n_past = n_pages * PAGE_SIZE
    kf = float(topk)
    knew = snew_ref[...]
    lane = lax.broadcasted_iota(I32, (db, LANES), 1)

    def count(pred_fn):
        def body(p, acc):
            return acc + jnp.where(pred_fn(sc_ref[p], p * PAGE_SIZE + lane), 1.0, 0.0)
        acc = lax.fori_loop(0, n_pages, body, jnp.zeros((db, LANES), F32))
        tot = jnp.broadcast_to(jnp.sum(acc, axis=1, keepdims=True), (db, LANES))
        return tot + jnp.where(pred_fn(knew, n_past), 1.0, 0.0)

    t = _kth_largest_threshold(lambda c: count(lambda kk, ii: kk >= c), (db, LANES), kf)
    need = kf - count(lambda kk, ii: kk > t)
    excess = jnp.where(count(lambda kk, ii: kk >= t) > kf, 1.0, 0.0)

    def tie_limit():
        def jbody(i, j):
            cand = j | jnp.left_shift(jnp.int32(1), idx_bits - 1 - i)
            c = count(lambda kk, ii: (kk == t) & (ii < cand))
            return jnp.where(c <= need, cand, j)
        return lax.fori_loop(0, idx_bits, jbody, jnp.zeros((db, LANES), I32))

    jlim = lax.cond(jnp.max(excess) > 0.0, tie_limit,
                    lambda: jnp.full((db, LANES), 2 ** idx_bits - 1, I32))

    def sel_body(p, carry):
        kk = sc_ref[p]
        sel_ref[p] = jnp.where((kk > t) | ((kk == t) & ((p * PAGE_SIZE + lane) < jlim)), 1.0, 0.0)
        return carry

    lax.fori_loop(0, n_pages, sel_body, 0)
    selnew_ref[...] = jnp.where((knew > t) | ((knew == t) & (n_past < jlim)), 1.0, 0.0)


def _topk_sample(scores_t, s_new):
    n_pages, db, _ = scores_t.shape
    length = n_pages * PAGE_SIZE + 1
    topk = min(IDX_TOPK, length // 4)
    idx_bits = max(1, int(math.ceil(math.log2(length + 1))))
    return pl.pallas_call(
        functools.partial(_topk_sample_kernel, topk=topk, idx_bits=idx_bits),
        out_shape=[jax.ShapeDtypeStruct((n_pages, db, PAGE_SIZE), F32),
                   jax.ShapeDtypeStruct((db, LANES), F32)],
        compiler_params=pltpu.CompilerParams(vmem_limit_bytes=VMEM_LIMIT),
        name="topk_sample",
    )(scores_t, s_new)


def _dsa_sample_kernel(pt_ref, kt_hbm, vt_hbm, qt_ref, q8_ref, sel_ref, selnew_ref, knew_ref, vnew_ref,
                       o_ref, kbuf, vbuf, qbc, s_ref, m_ref, l_ref, acc_ref, sem, *, ch):
    b = pl.program_id(0)
    c = pl.program_id(1)
    nb = pl.num_programs(0)
    nch = pl.num_programs(1)
    step = b * nch + c
    slot = step % 2

    def copies(bb, cc, sl, j):
        page = pt_ref[bb, cc * ch + j]
        return (pltpu.make_async_copy(kt_hbm.at[page], kbuf.at[sl, j], sem.at[0, sl]),
                pltpu.make_async_copy(vt_hbm.at[page], vbuf.at[sl, j], sem.at[1, sl]))

    def fetch(bb, cc, sl):
        for j in range(ch):
            ck, cv = copies(bb, cc, sl, j)
            ck.start()
            cv.start()

    @pl.when(step == 0)
    def _():
        fetch(0, 0, 0)

    @pl.when(step + 1 < nb * nch)
    def _():
        nxt = step + 1
        fetch(nxt // nch, nxt % nch, 1 - slot)

    @pl.when(c == 0)
    def _():
        m_ref[...] = jnp.full(m_ref.shape, NEG_INF, F32)
        l_ref[...] = jnp.zeros(l_ref.shape, F32)
        acc_ref[...] = jnp.zeros(acc_ref.shape, F32)
        for h in range(A_HEADS):
            qbc[h] = jnp.broadcast_to(qt_ref[0, h], (A_HEAD_DIM, PAGE_SIZE))

    for j in range(ch):
        ck, cv = copies(b, c, slot, j)
        ck.wait()
        cv.wait()

    for j in range(ch):
        bias = jnp.where(sel_ref[0, j:j + 1, :] > 0.5, 0.0, NEG_INF)
        for h in range(A_HEADS):
            s = jnp.sum(kbuf[slot, j, h // 2] * qbc[h], axis=0, keepdims=True)
            s_ref[j, h:h + 1, :] = s + bias
    s_all = s_ref[...]
    mc = jnp.max(jnp.max(s_all, axis=0), axis=1, keepdims=True)
    m_old = m_ref[...]
    m_new = jnp.maximum(m_old, mc)
    m_safe = jnp.where(m_new == NEG_INF, 0.0, m_new)
    alpha = jnp.exp2(m_old - m_safe)
    m_ref[...] = m_new
    p_all = jnp.exp2(s_all - m_safe[None])
    l_ref[...] = alpha * l_ref[...] + jnp.sum(jnp.sum(p_all, axis=0), axis=1, keepdims=True)
    s_ref[...] = p_all
    for h in range(A_HEADS):
        a = acc_ref[h] * alpha[h:h + 1, :]
        for j in range(ch):
            a = a + vbuf[slot, j, h // 2] * s_ref[j, h:h + 1, :]
        acc_ref[h] = a

    @pl.when(c == nch - 1)
    def _():
        m_old = m_ref[...]
        s_new = jnp.sum(q8_ref[0] * knew_ref[0], axis=1, keepdims=True)
        s_new = jnp.where(selnew_ref[0, :, 0:1] > 0.5, s_new, NEG_INF)
        m_new = jnp.maximum(m_old, s_new)
        alpha = jnp.exp2(m_old - m_new)
        p_new = jnp.exp2(s_new - m_new)
        denom = alpha * l_ref[...] + p_new
        for h in range(A_HEADS):
            num = (jnp.sum(acc_ref[h], axis=1, keepdims=True) * alpha[h:h + 1, :]
                   + vnew_ref[0, h] * p_new[h:h + 1, :])
            o_ref[0, h] = num / denom[h:h + 1, :]


def _dsa_sample(page_table, kt, vt, qt, q8, sel, sel_new, knew8, vnew8t, ch):
    db, n_pages = page_table.shape
    nch = n_pages // ch
    hd = A_HEAD_DIM
    grid_spec = pltpu.PrefetchScalarGridSpec(
        num_scalar_prefetch=1,
        grid=(db, nch),
        in_specs=[pl.BlockSpec(memory_space=pl.ANY), pl.BlockSpec(memory_space=pl.ANY),
                  pl.BlockSpec((1, A_HEADS, hd, 1), lambda b, c, *_: (b, 0, 0, 0)),
                  pl.BlockSpec((1, A_HEADS, hd), lambda b, c, *_: (b, 0, 0)),
                  pl.BlockSpec((1, ch, PAGE_SIZE), lambda b, c, *_: (b, c, 0)),
                  pl.BlockSpec((1, 1, LANES), lambda b, c, *_: (b, 0, 0)),
                  pl.BlockSpec((1, A_HEADS, hd), lambda b, c, *_: (b, 0, 0)),
                  pl.BlockSpec((1, A_HEADS, hd, 1), lambda b, c, *_: (b, 0, 0, 0))],
        out_specs=pl.BlockSpec((1, A_HEADS, hd, 1), lambda b, c, *_: (b, 0, 0, 0)),
        scratch_shapes=[pltpu.VMEM((2, ch, A_KV_HEADS, hd, PAGE_SIZE), F32),
                        pltpu.VMEM((2, ch, A_KV_HEADS, hd, PAGE_SIZE), F32),
                        pltpu.VMEM((A_HEADS, hd, PAGE_SIZE), F32),
                        pltpu.VMEM((ch, A_HEADS, PAGE_SIZE), F32),
                        pltpu.VMEM((A_HEADS, 1), F32), pltpu.VMEM((A_HEADS, 1), F32),
                        pltpu.VMEM((A_HEADS, hd, PAGE_SIZE), F32),
                        pltpu.SemaphoreType.DMA((2, 2))])
    return pl.pallas_call(
        functools.partial(_dsa_sample_kernel, ch=ch),
        grid_spec=grid_spec,
        out_shape=jax.ShapeDtypeStruct((db, A_HEADS, hd, 1), F32),
        compiler_params=pltpu.CompilerParams(dimension_semantics=("arbitrary", "arbitrary"),
                                             vmem_limit_bytes=VMEM_LIMIT),
        name="dsa_sample",
    )(page_table, kt, vt, qt, q8, sel, sel_new, knew8, vnew8t)


def _qabs_kernel(qb_ref, wukt_ref, o_ref):
    for h in range(B_HEADS):
        o_ref[h] = _dot(qb_ref[:, h * LANES:(h + 1) * LANES], wukt_ref[h]).astype(BF16)


def _qabs(q_b, wukt):
    db = q_b.shape[0]
    return pl.pallas_call(
        _qabs_kernel,
        out_shape=jax.ShapeDtypeStruct((B_HEADS, db, B_KV_LORA), BF16),
        name="mla_qabs",
    )(q_b, wukt)


def _ouv_kernel(olat_ref, wuv_ref, o_ref):
    db = olat_ref.shape[1]
    lane = lax.broadcasted_iota(I32, (db, LANES), 1)
    for c in range(B_HEADS // 2):
        w = wuv_ref[:, c * LANES:(c + 1) * LANES]
        r0 = _dot(olat_ref[2 * c].astype(BF16), w)
        r1 = _dot(olat_ref[2 * c + 1].astype(BF16), w)
        o_ref[:, c * LANES:(c + 1) * LANES] = jnp.where(lane < 64, r0, r1)


def _ouv(o_lat, wuv):
    db = o_lat.shape[1]
    return pl.pallas_call(
        _ouv_kernel,
        out_shape=jax.ShapeDtypeStruct((db, B_HEADS * B_VDIM), F32),
        name="mla_ouv",
    )(o_lat, wuv)


def _mla_sample_kernel(pt_ref, ckv_hbm, krt_hbm, qabs_ref, qrt_ref, qr_ref, cnew_ref, krnew_ref,
                       o_ref, cbuf, rbuf, qrbc, sr_ref, m_ref, l_ref, acc_ref, sem, *, ch):
    b = pl.program_id(0)
    c = pl.program_id(1)
    nb = pl.num_programs(0)
    nch = pl.num_programs(1)
    step = b * nch + c
    slot = step % 2

    def copies(bb, cc, sl, j):
        page = pt_ref[bb, cc * ch + j]
        return (pltpu.make_async_copy(ckv_hbm.at[page], cbuf.at[sl, pl.ds(j * PAGE_SIZE, PAGE_SIZE)],
                                      sem.at[0, sl]),
                pltpu.make_async_copy(krt_hbm.at[page], rbuf.at[sl, j], sem.at[1, sl]))

    def fetch(bb, cc, sl):
        for j in range(ch):
            cc_, cr_ = copies(bb, cc, sl, j)
            cc_.start()
            cr_.start()

    @pl.when(step == 0)
    def _():
        fetch(0, 0, 0)

    @pl.when(step + 1 < nb * nch)
    def _():
        nxt = step + 1
        fetch(nxt // nch, nxt % nch, 1 - slot)

    @pl.when(c == 0)
    def _():
        m_ref[...] = jnp.full(m_ref.shape, NEG_INF, F32)
        l_ref[...] = jnp.zeros(l_ref.shape, F32)
        acc_ref[...] = jnp.zeros(acc_ref.shape, F32)
        for h in range(B_HEADS):
            qrbc[h] = jnp.broadcast_to(qrt_ref[0, h], (B_ROPE, PAGE_SIZE))

    for j in range(ch):
        cc_, cr_ = copies(b, c, slot, j)
        cc_.wait()
        cr_.wait()

    cb = cbuf[slot].astype(BF16)
    s_lat = _dot_nt(qabs_ref[0], cb)
    for j in range(ch):
        blk = rbuf[slot, j]
        for h in range(B_HEADS):
            sr_ref[h:h + 1, j * PAGE_SIZE:(j + 1) * PAGE_SIZE] = jnp.sum(blk * qrbc[h], axis=0, keepdims=True)
    s = s_lat + sr_ref[...]
    m_old = m_ref[...]
    m_new = jnp.maximum(m_old, jnp.max(s, axis=1, keepdims=True))
    alpha = jnp.exp2(m_old - m_new)
    p = jnp.exp2(s - m_new)
    l_ref[...] = alpha * l_ref[...] + jnp.sum(p, axis=1, keepdims=True)
    acc_ref[...] = alpha * acc_ref[...] + _dot(p.astype(BF16), cb)
    m_ref[...] = m_new

    @pl.when(c == nch - 1)
    def _():
        m_old = m_ref[...]
        s_new = (jnp.sum(qabs_ref[0].astype(F32) * cnew_ref[0], axis=1, keepdims=True)
                 + jnp.sum(qr_ref[0] * krnew_ref[0], axis=1, keepdims=True))
        m_new = jnp.maximum(m_old, s_new)
        alpha = jnp.exp2(m_old - m_new)
        p_new = jnp.exp2(s_new - m_new)
        denom = alpha * l_ref[...] + p_new
        o_ref[0] = (alpha * acc_ref[...] + p_new * cnew_ref[0]) / denom


def _mla_sample(page_table, ckv_pages, krt, qabs, qrt, qr, c_new, kr_new, ch):
    db, n_pages = page_table.shape
    nch = n_pages // ch
    grid_spec = pltpu.PrefetchScalarGridSpec(
        num_scalar_prefetch=1,
        grid=(db, nch),
        in_specs=[pl.BlockSpec(memory_space=pl.ANY), pl.BlockSpec(memory_space=pl.ANY),
                  pl.BlockSpec((1, B_HEADS, B_KV_LORA), lambda b, c, *_: (b, 0, 0)),
                  pl.BlockSpec((1, B_HEADS, B_ROPE, 1), lambda b, c, *_: (b, 0, 0, 0)),
                  pl.BlockSpec((1, B_HEADS, B_ROPE), lambda b, c, *_: (b, 0, 0)),
                  pl.BlockSpec((1, 1, B_KV_LORA), lambda b, c, *_: (b, 0, 0)),
                  pl.BlockSpec((1, 1, B_ROPE), lambda b, c, *_: (b, 0, 0))],
        out_specs=pl.BlockSpec((1, B_HEADS, B_KV_LORA), lambda b, c, *_: (b, 0, 0)),
        scratch_shapes=[pltpu.VMEM((2, ch * PAGE_SIZE, B_KV_LORA), F32),
                        pltpu.VMEM((2, ch, B_ROPE, PAGE_SIZE), F32),
                        pltpu.VMEM((B_HEADS, B_ROPE, PAGE_SIZE), F32),
                        pltpu.VMEM((B_HEADS, ch * PAGE_SIZE), F32),
                        pltpu.VMEM((B_HEADS, 1), F32), pltpu.VMEM((B_HEADS, 1), F32),
                        pltpu.VMEM((B_HEADS, B_KV_LORA), F32),
                        pltpu.SemaphoreType.DMA((2, 2))])
    return pl.pallas_call(
        functools.partial(_mla_sample_kernel, ch=ch),
        grid_spec=grid_spec,
        out_shape=jax.ShapeDtypeStruct((db, B_HEADS, B_KV_LORA), F32),
        compiler_params=pltpu.CompilerParams(dimension_semantics=("arbitrary", "arbitrary"),
                                             vmem_limit_bytes=VMEM_LIMIT),
        name="mla_sample",
    )(page_table, ckv_pages, krt, qabs, qrt, qr, c_new, kr_new)


def _rope_tables(pos):
    pos = pos.astype(F32)[:, None]
    n = pos.shape[0]

    def cs(half):
        inv = jnp.power(ROPE_THETA, -jnp.arange(half, dtype=F32) / half)
        ang = pos * inv[None, :]
        return jnp.cos(ang), jnp.sin(ang)

    c, s = cs(A_HEAD_DIM // 2)
    c64 = jnp.tile(jnp.concatenate([c, c], axis=1), (1, 2))
    s64 = jnp.tile(jnp.concatenate([-s, s], axis=1), (1, 2))
    c, s = cs(B_ROPE // 2)
    one, zero = jnp.ones((n, B_NOPE), F32), jnp.zeros((n, B_NOPE), F32)
    cq = jnp.concatenate([one, c, c, one[:, :32]], axis=1)
    sq = jnp.concatenate([zero, -s, s, zero[:, :32]], axis=1)
    return c64, s64, cq, sq


def _pack_weights(norm_w, w_in, qn_a, kn_a, q_lora_norm, w_uq, qn_b, kv_lora_norm, kn_b, w_uk, w_uv,
                  w_pa, w_pb, w_o):
    d = w_in.shape[0]
    sizes = (512, 256, 256, 512, IDX_DIM, IDX_HEADS, 512, B_Q_LORA, B_KV_LORA, B_ROPE, 512, d, d)
    offs = [0]
    for n in sizes:
        offs.append(offs[-1] + n)
    (w_qa, w_ka, w_va, w_iq, w_ik, w_iw, w_za, w_cq, w_ckv, w_kr, w_zb, w_ga, w_gb) = [
        w_in[:, offs[i]:offs[i + 1]] for i in range(len(sizes))]
    zc = lambda n: jnp.zeros((d, n), F32)
    w_all = jnp.concatenate(
        [w_qa, w_ka, w_va, w_iq, w_ik, w_ik, w_iw, zc(LANES - IDX_HEADS), w_cq, w_ckv,
         zc(B_NOPE), w_kr, zc(LANES - B_NOPE - B_ROPE)], axis=1).astype(BF16)
    perm = jnp.array(O_PERM)
    w_za_p = w_za.reshape(d, A_HEADS, A_HEAD_DIM)[:, perm].reshape(d, 512)
    wg = jnp.concatenate([w_za_p, w_zb, w_ga, w_gb], axis=1).astype(BF16)
    wpa = w_pa.reshape(A_HEADS, A_HEAD_DIM, d)[perm].reshape(512, d).astype(BF16)
    wuq3 = w_uq.reshape(B_Q_LORA, B_HEADS, B_QK)
    wuq = jnp.pad(wuq3, ((0, 0), (0, 0), (0, LANES - B_QK))).reshape(B_Q_LORA, B_HEADS * LANES).astype(BF16)
    wuk = jnp.pad(w_uk, ((0, 0), (0, 0), (0, LANES - B_NOPE))).reshape(B_KV_LORA, B_HEADS * LANES).astype(BF16)
    wukt = jnp.pad(jnp.transpose(w_uk, (1, 2, 0)), ((0, 0), (0, LANES - B_NOPE), (0, 0))).astype(BF16)
    wuv = w_uv.reshape(B_KV_LORA, B_HEADS * B_VDIM).astype(BF16)
    z32 = jnp.zeros((32,), F32)
    r = jnp.arange(LANES)
    return dict(
        norm_w=norm_w.reshape(1, d), w_all=w_all, wg=wg, wpa=wpa, wpb=w_pb.astype(BF16), wo=w_o.astype(BF16),
        gq=jnp.tile(qn_a, 2).reshape(1, LANES), gk=jnp.tile(kn_a, 2).reshape(1, LANES),
        gcq=q_lora_norm.reshape(1, B_Q_LORA), gckv=kv_lora_norm.reshape(1, B_KV_LORA),
        gqb=jnp.concatenate([qn_b, z32]).reshape(1, LANES),
        gkr=jnp.concatenate([jnp.zeros((B_NOPE,), F32), kn_b, z32]).reshape(1, LANES),
        wuq=wuq, wuk=wuk, wukt=wukt, wuv=wuv,
        bd64=jnp.tile((r[:, None] // 64 == r[None, :] // 64).astype(BF16), (2, 1)),
        ones=jnp.ones((2 * LANES, LANES), BF16))


def _tiles(seq, n_pages):
    return dict(tq=min(256, seq), dsa_pages=min(32, n_pages), mla_pages=min(32, n_pages))


def _layer(x_p, x_s, cache_k, cache_v, cache_ik, cache_ckv, cache_kr, page_table, wts):
    bsz, seq, d = x_p.shape
    db, dec_seq, _ = x_s.shape
    assert dec_seq == 1, "the sample group decodes one token per sequence"
    n_pages = page_table.shape[1]
    n_past = n_pages * PAGE_SIZE
    tiles = _tiles(seq, n_pages)
    tq = tiles["tq"]

    tabs_p = _rope_tables(jnp.arange(seq, dtype=I32))
    pr = _project(x_p, tabs_p, wts, tq)
    r3 = lambda a: a.reshape(bsz, seq, a.shape[-1])
    chunked = lambda a: a.reshape((bsz, seq // tq) + a.shape[1:])
    iwt = jnp.swapaxes(r3(pr["iw_f"])[:, :, :IDX_HEADS], 1, 2)
    o_a = _dsa_prompt(r3(pr["q_arr"]), r3(pr["iq"]), iwt, r3(pr["k_bf"]), chunked(pr["vtc"]), r3(pr["ikd"]), tq)
    o_b = _mla_prompt(r3(pr["q_b"]), r3(pr["k_b"]), chunked(pr["vbtc"]), tq)
    y_p = _merge(x_p.reshape(bsz * seq, d), o_a.reshape(bsz * seq, 512), o_b.reshape(bsz * seq, 512),
                 wts, tq).reshape(bsz, seq, d)

    tabs_s = _rope_tables(jnp.full((db,), n_past, dtype=I32))
    ps = _project(x_s.reshape(1, db, d), tabs_s, wts, db)
    for name in ("k", "v", "ik", "kr"):
        ps[name + "_f"] = ps[name + "_t"][0].T
    ikt = jnp.swapaxes(cache_ik, 1, 2)
    krt = jnp.swapaxes(cache_kr, 1, 2)
    kt = jnp.transpose(cache_k, (0, 2, 3, 1))
    vt = jnp.transpose(cache_v, (0, 2, 3, 1))
    iq3 = ps["iq"].astype(F32).reshape(db, IDX_HEADS, IDX_DIM)
    iw_col = ps["iw_f"][:, :IDX_HEADS, None]
    scores, s_new = _idx_scores(page_table, ikt, iq3, ps["ik_f"].reshape(db, 1, IDX_DIM), iw_col)
    scores_t = jnp.swapaxes(scores.reshape(db, n_pages, PAGE_SIZE), 0, 1)
    sel_t, sel_new = _topk_sample(scores_t, s_new.reshape(db, LANES))
    sel = jnp.swapaxes(sel_t, 0, 1)
    q4 = ps["q_arr"].astype(F32).reshape(db, A_HEADS, 2, 64)
    q8 = q4[:, :, 0] + q4[:, :, 1]
    knew8 = jnp.repeat(ps["k_f"].reshape(db, A_KV_HEADS, A_HEAD_DIM), A_HEADS // A_KV_HEADS, axis=1)
    vnew8 = jnp.repeat(ps["v_f"].reshape(db, A_KV_HEADS, A_HEAD_DIM), A_HEADS // A_KV_HEADS, axis=1)
    o_as = _dsa_sample(page_table, kt, vt, q8[..., None], q8, sel, sel_new.reshape(db, 1, LANES), knew8,
                       vnew8[..., None], ch=tiles["dsa_pages"])
    o_as = o_as.reshape(db, A_HEADS, A_HEAD_DIM)[:, jnp.array(O_PERM)].reshape(db, 512)

    qabs = jnp.swapaxes(_qabs(ps["q_b"], wts["wukt"]), 0, 1)
    qr = ps["q_b"].astype(F32).reshape(db, B_HEADS, LANES)[:, :, B_NOPE:B_NOPE + B_ROPE]
    o_lat = _mla_sample(page_table, cache_ckv, krt, qabs, qr[..., None], qr,
                        ps["ckv_f"].reshape(db, 1, B_KV_LORA), ps["kr_f"].reshape(db, 1, B_ROPE),
                        ch=tiles["mla_pages"])
    o_bs = _ouv(jnp.swapaxes(o_lat, 0, 1), wts["wuv"])
    y_s = _merge(x_s.reshape(db, d), o_as, o_bs, wts, db).reshape(db, 1, d)

    heads_t = lambda a: jnp.transpose(a.reshape(bsz, A_KV_HEADS, A_HEAD_DIM, seq), (0, 3, 1, 2))
    new_p = (heads_t(pr["k_t"]), heads_t(pr["v_t"]), jnp.swapaxes(pr["ik_t"], 1, 2),
             pr["ckv_f"].reshape(bsz, seq, B_KV_LORA), jnp.swapaxes(pr["kr_t"], 1, 2))
    new_s = (ps["k_f"].reshape(db, 1, A_KV_HEADS, A_HEAD_DIM), ps["v_f"].reshape(db, 1, A_KV_HEADS, A_HEAD_DIM),
             ps["ik_f"].reshape(db, 1, IDX_DIM), ps["ckv_f"].reshape(db, 1, B_KV_LORA),
             ps["kr_f"].reshape(db, 1, B_ROPE))
    return y_p, y_s, new_p, new_s


def kernel(x_prompt, x_sample, cache_k, cache_v, cache_idx_k, cache_ckv, cache_kr, page_table, norm_w, w_in, qn_a, kn_a, q_lora_norm, w_uq, qn_b, kv_lora_norm, kn_b, w_uk, w_uv, w_pa, w_pb, w_o):
    depth = w_in.shape[0]
    x_p, x_s = x_prompt, x_sample
    outs_p, outs_s = [], []
    for l in range(depth):
        wts = _pack_weights(norm_w[l], w_in[l], qn_a[l], kn_a[l], q_lora_norm[l], w_uq[l], qn_b[l],
                            kv_lora_norm[l], kn_b[l], w_uk[l], w_uv[l], w_pa[l], w_pb[l], w_o[l])
        x_p, x_s, new_p, new_s = _layer(x_p, x_s, cache_k[l], cache_v[l], cache_idx_k[l], cache_ckv[l],
                                        cache_kr[l], page_table, wts)
        outs_p.append(new_p)
        outs_s.append(new_s)
    stack = lambda outs, i: jnp.stack([o[i] for o in outs])
    return (x_p, x_s) + tuple(stack(outs_p, i) for i in range(5)) + tuple(stack(outs_s, i) for i in range(5))
```
